```python
import jax, jax.numpy as jnp
from jax import lax
import numpy as np

D_MODEL = 2048
BATCH = 4
SEQ = 4096
DEPTH = 1
DEC_BATCH = 128
DEC_SEQ = 8
PAST_LEN = 16384
PAGE_SIZE = 128

N_HEADS = 16
QK_NOPE = 128
QK_ROPE = 64
QK_DIM = QK_NOPE + QK_ROPE
V_DIM = 128
Q_LORA = 512
KV_LORA = 512
ROPE_THETA = 10000.0
Q_BLOCK = 128
POOL_WIDTH = 1024
POOL_WINDOWS = (2, 4, 8, 16)
N_POOL_GROUPS = 4
POOL_GROUP = POOL_WIDTH // N_POOL_GROUPS
POOL_PAD = max(POOL_WINDOWS) - 1
PEER_HEADS = 8
N_KEYS = 128
N_EXPERTS = N_KEYS * N_KEYS
PEER_TOPK = 16
PEER_QDIM = 256
PEER_HALF = PEER_QDIM // 2
PEER_BLOCK = 128
PLE_DIM = 256
EPS = 1e-6
IN_SIZES = (Q_LORA, KV_LORA, QK_ROPE, POOL_WIDTH, D_MODEL, D_MODEL)
IN_DIM = sum(IN_SIZES)

kernel_name = "mla_pool_peer_gated_hybrid_step"


def rmsnorm(x, g):
    xf = x.astype(jnp.float32)
    y = xf * lax.rsqrt(jnp.mean(xf * xf, axis=-1, keepdims=True) + EPS) * g.astype(jnp.float32)
    return y.astype(x.dtype)


def rope(x, pos):
    half = QK_ROPE // 2
    freq = ROPE_THETA ** (-jnp.arange(half, dtype=jnp.float32) / half)
    ang = pos.astype(jnp.float32)[:, None] * freq[None, :]
    cos = jnp.cos(ang)[:, None, :]
    sin = jnp.sin(ang)[:, None, :]
    xf = x.astype(jnp.float32)
    x1, x2 = xf[..., :half], xf[..., half:]
    return jnp.concatenate([x1 * cos - x2 * sin, x2 * cos + x1 * sin], axis=-1).astype(x.dtype)


def project_in(x, norm1_g, w_in, kv_norm_g):
    h = rmsnorm(x, norm1_g)
    z = h @ w_in
    idx = np.cumsum(IN_SIZES)[:-1].tolist()
    q_lat, c_raw, pe_raw, u_pool, g_a, g_b = jnp.split(z, idx, axis=-1)
    c_lat = rmsnorm(c_raw, kv_norm_g)
    return q_lat, c_lat, pe_raw, u_pool, g_a, g_b


def mla_queries(q_lat, q_norm_g, w_q_up, q_head_g, pos):
    q = rmsnorm(q_lat, q_norm_g) @ w_q_up
    q = q.reshape(q.shape[:-1] + (N_HEADS, QK_DIM))
    q = rmsnorm(q, q_head_g)
    return jnp.concatenate([q[..., :QK_NOPE], rope(q[..., QK_NOPE:], pos)], axis=-1)


def mla_keys_values(c_lat, pe_raw, w_kv_up, k_head_g, pos):
    kv = (c_lat @ w_kv_up).reshape(c_lat.shape[:-1] + (N_HEADS, QK_NOPE + V_DIM))
    k_nope, v = kv[..., :QK_NOPE], kv[..., QK_NOPE:]
    pe = jnp.broadcast_to(pe_raw[..., None, :], pe_raw.shape[:-1] + (N_HEADS, QK_ROPE)).astype(k_nope.dtype)
    k = rmsnorm(jnp.concatenate([k_nope, pe], axis=-1), k_head_g)
    k = jnp.concatenate([k[..., :QK_NOPE], rope(k[..., QK_NOPE:], pos)], axis=-1)
    return k, v


def attend(q, k, v, q_pos, k_pos):
    s = jnp.einsum('...qhd,...khd->...hqk', q, k).astype(jnp.float32) * (QK_DIM ** -0.5)
    s = jnp.where(k_pos[None, :] <= q_pos[:, None], s, -jnp.inf)
    p = jax.nn.softmax(s, axis=-1).astype(v.dtype)
    return jnp.einsum('...hqk,...khd->...qhd', p, v)


def pool_mix(u, buf, start_pos, w_pool_map, pool_scale):
    B, T, _ = u.shape
    ext = jnp.concatenate([buf.astype(u.dtype), u], axis=1)
    extf = ext.astype(jnp.float32)
    cs = jnp.concatenate([jnp.zeros((B, 1, POOL_WIDTH), jnp.float32), jnp.cumsum(extf, axis=1)], axis=1)
    end = cs[:, POOL_PAD + 1:]
    t = jnp.arange(T)
    groups = []
    for gi, w in enumerate(POOL_WINDOWS):
        sl = slice(gi * POOL_GROUP, (gi + 1) * POOL_GROUP)
        s = end[..., sl] - cs[:, POOL_PAD + 1 - w: POOL_PAD + 1 - w + T, sl]
        cnt = jnp.minimum(w, start_pos + t + 1).astype(jnp.float32)
        groups.append(s / cnt[None, :, None])
    d = (jnp.concatenate(groups, axis=-1) - u.astype(jnp.float32)).astype(u.dtype)
    d = d.reshape(B, T, N_POOL_GROUPS, POOL_GROUP)
    y = jnp.einsum('btgc,gcd->btgd', d, w_pool_map).reshape(B, T, POOL_WIDTH) * pool_scale
    return y, ext[:, -POOL_PAD:]


def peer(h, w_peer_q, peer_subkeys, peer_u, peer_v):
    shp = h.shape
    x = h.reshape(-1, D_MODEL)
    n = x.shape[0]
    nb = -(-n // PEER_BLOCK)
    x = jnp.pad(x, ((0, nb * PEER_BLOCK - n), (0, 0)))

    def block(xb):
        q = (xb @ w_peer_q).reshape(PEER_BLOCK, PEER_HEADS, 2, PEER_HALF)
        s = jnp.einsum('thpc,hpkc->thpk', q.astype(jnp.float32), peer_subkeys.astype(jnp.float32))
        s_top, i_top = lax.top_k(s, PEER_TOPK)
        cand = s_top[:, :, 0, :, None] + s_top[:, :, 1, None, :]
        cand_idx = i_top[:, :, 0, :, None] * N_KEYS + i_top[:, :, 1, None, :]
        best, pos = lax.top_k(cand.reshape(PEER_BLOCK, PEER_HEADS, PEER_TOPK * PEER_TOPK), PEER_TOPK)
        idx = jnp.take_along_axis(cand_idx.reshape(PEER_BLOCK, PEER_HEADS, PEER_TOPK * PEER_TOPK), pos, axis=-1)
        g = jax.nn.softmax(best, axis=-1)
        act = jax.nn.gelu(jnp.einsum('thkd,td->thk', peer_u[idx], xb).astype(jnp.float32), approximate=False)
        coef = (g * act).astype(xb.dtype)
        return jnp.einsum('thk,thkd->td', coef, peer_v[idx])

    y = lax.map(block, x.reshape(nb, PEER_BLOCK, D_MODEL))
    return y.reshape(-1, D_MODEL)[:n].reshape(shp)


def merge_and_channel(x, attn_o, pool_y, g_a, g_b, p, w_a_proj, w_b_proj, w_out, norm2_g,
                      w_peer_q, peer_subkeys, peer_u, peer_v, norm3_g, w_ple, w_ple_gate):
    a = attn_o @ w_a_proj
    b = pool_y @ w_b_proj
    m = jax.nn.sigmoid(g_a) * a + jax.nn.sigmoid(g_b) * b
    x1 = x + m @ w_out
    x2 = x1 + peer(rmsnorm(x1, norm2_g), w_peer_q, peer_subkeys, peer_u, peer_v)
    gate = jax.nn.sigmoid(rmsnorm(x2, norm3_g) @ w_ple_gate)
    return x2 + (p @ w_ple) * gate


def setup_inputs(seed: int = 0) -> dict:
    key = jax.random.key(seed)
    ks = iter(jax.random.split(key, 48))
    n_pages = PAST_LEN // PAGE_SIZE
    n_used = DEC_BATCH * n_pages
    n_phys = n_used + max(1, n_used // 4)
    f32 = jnp.float32

    def nrm(shape, scale):
        return jax.random.normal(next(ks), shape, f32) * scale

    def gain(dim):
        return 1.0 + nrm((DEPTH, dim), 0.1)

    page_table = jax.random.permutation(next(ks), n_phys)[:n_used].reshape(DEC_BATCH, n_pages).astype(jnp.int32)
    return {
        "x_prompt": nrm((BATCH, SEQ, D_MODEL), 1.0),
        "x_sample": nrm((DEC_BATCH, DEC_SEQ, D_MODEL), 1.0),
        "p_prompt": nrm((DEPTH, BATCH, SEQ, PLE_DIM), 1.0),
        "p_sample": nrm((DEPTH, DEC_BATCH, DEC_SEQ, PLE_DIM), 1.0),
        "cache_kv_latent": nrm((DEPTH, n_phys, PAGE_SIZE, KV_LORA), 1.0),
        "cache_k_rope": nrm((DEPTH, n_phys, PAGE_SIZE, QK_ROPE), 1.0),
        "state_pool": nrm((DEPTH, DEC_BATCH, POOL_PAD, POOL_WIDTH), 1.0),
        "page_table": page_table,
        "norm1_g": gain(D_MODEL),
        "w_in": nrm((DEPTH, D_MODEL, IN_DIM), D_MODEL ** -0.5),
        "q_norm_g": gain(Q_LORA),
        "w_q_up": nrm((DEPTH, Q_LORA, N_HEADS * QK_DIM), Q_LORA ** -0.5),
        "kv_norm_g": gain(KV_LORA),
        "w_kv_up": nrm((DEPTH, KV_LORA, N_HEADS * (QK_NOPE + V_DIM)), KV_LORA ** -0.5),
        "q_head_g": gain(QK_DIM),
        "k_head_g": gain(QK_DIM),
        "w_pool_map": nrm((DEPTH, N_POOL_GROUPS, POOL_GROUP, POOL_GROUP), POOL_GROUP ** -0.5),
        "pool_scale": 1.0 + nrm((DEPTH, POOL_WIDTH), 0.1),
        "w_a_proj": nrm((DEPTH, N_HEADS * V_DIM, D_MODEL), (N_HEADS * V_DIM) ** -0.5),
        "w_b_proj": nrm((DEPTH, POOL_WIDTH, D_MODEL), POOL_WIDTH ** -0.5),
        "w_out": nrm((DEPTH, D_MODEL, D_MODEL), D_MODEL ** -0.5),
        "norm2_g": gain(D_MODEL),
        "w_peer_q": nrm((DEPTH, D_MODEL, PEER_HEADS * PEER_QDIM), D_MODEL ** -0.5),
        "peer_subkeys": nrm((DEPTH, PEER_HEADS, 2, N_KEYS, PEER_HALF), PEER_HALF ** -0.5),
        "peer_u": nrm((DEPTH, N_EXPERTS, D_MODEL), D_MODEL ** -0.5),
        "peer_v": nrm((DEPTH, N_EXPERTS, D_MODEL), PEER_HEADS ** -0.5),
        "norm3_g": gain(D_MODEL),
        "w_ple": nrm((DEPTH, PLE_DIM, D_MODEL), PLE_DIM ** -0.5),
        "w_ple_gate": nrm((DEPTH, D_MODEL, D_MODEL), D_MODEL ** -0.5),
    }


def reference(x_prompt, x_sample, p_prompt, p_sample, cache_kv_latent, cache_k_rope, state_pool, page_table,
              norm1_g, w_in, q_norm_g, w_q_up, kv_norm_g, w_kv_up, q_head_g, k_head_g, w_pool_map, pool_scale,
              w_a_proj, w_b_proj, w_out, norm2_g, w_peer_q, peer_subkeys, peer_u, peer_v, norm3_g, w_ple, w_ple_gate):
    B, S = x_prompt.shape[0], x_prompt.shape[1]
    DB, DS = x_sample.shape[0], x_sample.shape[1]
    pos_p = jnp.arange(S, dtype=jnp.int32)
    pos_s = PAST_LEN + jnp.arange(DS, dtype=jnp.int32)
    k_pos_s = jnp.arange(PAST_LEN + DS, dtype=jnp.int32)
    n_qb = S // Q_BLOCK

    xp, xs = x_prompt, x_sample
    lat_p, pe_p, pool_p, lat_s, pe_s, pool_s = [], [], [], [], [], []
    for i in range(DEPTH):
        q_lat, c_lat, pe_raw, u_pool, g_a, g_b = project_in(xp, norm1_g[i], w_in[i], kv_norm_g[i])
        q = mla_queries(q_lat, q_norm_g[i], w_q_up[i], q_head_g[i], pos_p)
        k, v = mla_keys_values(c_lat, pe_raw, w_kv_up[i], k_head_g[i], pos_p)
        qb = q.reshape(B, n_qb, Q_BLOCK, N_HEADS, QK_DIM).transpose(1, 0, 2, 3, 4)
        pb = pos_p.reshape(n_qb, Q_BLOCK)
        o = lax.map(lambda a: attend(a[0], k, v, a[1], pos_p), (qb, pb))
        o = o.transpose(1, 0, 2, 3, 4).reshape(B, S, N_HEADS * V_DIM)
        buf0 = jnp.zeros((B, POOL_PAD, POOL_WIDTH), xp.dtype)
        py, new_buf_p = pool_mix(u_pool, buf0, 0, w_pool_map[i], pool_scale[i])
        xp = merge_and_channel(xp, o, py, g_a, g_b, p_prompt[i], w_a_proj[i], w_b_proj[i], w_out[i], norm2_g[i],
                               w_peer_q[i], peer_subkeys[i], peer_u[i], peer_v[i], norm3_g[i], w_ple[i], w_ple_gate[i])
        lat_p.append(c_lat)
        pe_p.append(pe_raw)
        pool_p.append(new_buf_p)

        q_lat_s, c_lat_s, pe_raw_s, u_pool_s, g_a_s, g_b_s = project_in(xs, norm1_g[i], w_in[i], kv_norm_g[i])
        q_s = mla_queries(q_lat_s, q_norm_g[i], w_q_up[i], q_head_g[i], pos_s)

        def one_seq(args, i=i):
            q_b, pt_b, c_new, pe_new = args
            c_past = cache_kv_latent[i, pt_b].reshape(PAST_LEN, KV_LORA).astype(c_new.dtype)
            pe_past = cache_k_rope[i, pt_b].reshape(PAST_LEN, QK_ROPE).astype(pe_new.dtype)
            c_all = jnp.concatenate([c_past, c_new], axis=0)
            pe_all = jnp.concatenate([pe_past, pe_new], axis=0)
            kk, vv = mla_keys_values(c_all, pe_all, w_kv_up[i], k_head_g[i], k_pos_s)
            return attend(q_b, kk, vv, pos_s, k_pos_s)

        o_s = lax.map(one_seq, (q_s, page_table, c_lat_s, pe_raw_s)).reshape(DB, DS, N_HEADS * V_DIM)
        py_s, new_buf_s = pool_mix(u_pool_s, state_pool[i], PAST_LEN, w_pool_map[i], pool_scale[i])
        xs = merge_and_channel(xs, o_s, py_s, g_a_s, g_b_s, p_sample[i], w_a_proj[i], w_b_proj[i], w_out[i], norm2_g[i],
                               w_peer_q[i], peer_subkeys[i], peer_u[i], peer_v[i], norm3_g[i], w_ple[i], w_ple_gate[i])
        lat_s.append(c_lat_s)
        pe_s.append(pe_raw_s)
        pool_s.append(new_buf_s)

    return (xp, xs, jnp.stack(lat_p), jnp.stack(pe_p), jnp.stack(pool_p),
            jnp.stack(lat_s), jnp.stack(pe_s), jnp.stack(pool_s))
```

```python
import functools

import jax
import jax.numpy as jnp
import numpy as np
from jax import lax
from jax.experimental import pallas as pl
from jax.experimental.pallas import tpu as pltpu

F32 = jnp.float32
BF16 = jnp.bfloat16

N_HEADS = 16
QK_NOPE = 128
QK_ROPE = 64
QK_DIM = QK_NOPE + QK_ROPE
V_DIM = 128
HEAD_PAD = 256
ROPE_THETA = 10000.0
POOL_WINDOWS = (2, 4, 8, 16)
POOL_GROUP = 256
POOL_HIST = 16
PEER_HEADS = 8
N_KEYS = 128
PEER_TOPK = 16
PEER_HALF = 128
PEER_BLOCK = 128
EPS = 1e-6
NEG_BIG = -1e30

VMEM_LIMIT_BYTES = 56 * 1024 * 1024
_NT = (((1,), (1,)), ((), ()))


def _cparams(n_axes):
    return pltpu.CompilerParams(dimension_semantics=("arbitrary",) * n_axes,
                                vmem_limit_bytes=VMEM_LIMIT_BYTES)


def _fused_mm_body(*refs, has_norm, has_a2, n_extras, epilogue, has_side):
    it = iter(refs)
    a1_ref = next(it)
    g1_ref = next(it) if has_norm else None
    w1_ref = next(it)
    a2_ref = next(it) if has_a2 else None
    w2_ref = next(it) if has_a2 else None
    extra_refs = [next(it) for _ in range(n_extras)]
    out_ref = next(it)
    side_ref = next(it) if has_side else None
    h_ref = next(it) if has_norm else None

    if has_norm:
        @pl.when(pl.program_id(1) == 0)
        def _():
            x = a1_ref[...].astype(F32)
            ms = jnp.mean(x * x, axis=-1, keepdims=True)
            h = x * lax.rsqrt(ms + EPS) * g1_ref[...]
            h_ref[...] = h.astype(BF16)
            if has_side:
                side_ref[...] = h.astype(side_ref.dtype)

        lhs = h_ref[...]
    else:
        lhs = a1_ref[...].astype(BF16)
    accs = [jnp.dot(lhs, w1_ref[...], preferred_element_type=F32)]
    if has_a2:
        accs.append(jnp.dot(a2_ref[...].astype(BF16), w2_ref[...], preferred_element_type=F32))
    out_ref[...] = epilogue(*accs, *[r[...] for r in extra_refs]).astype(out_ref.dtype)


def fused_mm(a1, w1, *, epilogue, out_dtype, bm, bn, g1=None, a1_cols=None, a2=None, w2=None,
             a2_cols=None, extras=(), side_dtype=None, name="fused_mm"):
    m = a1.shape[0]
    k1, n = w1.shape
    assert m % bm == 0 and n % bn == 0
    a1_idx = 0 if a1_cols is None else a1_cols[1]
    assert (a1.shape[1] if a1_cols is None else a1_cols[0]) == k1
    has_norm, has_a2, has_side = g1 is not None, a2 is not None, side_dtype is not None
    in_specs = [pl.BlockSpec((bm, k1), lambda i, j: (i, a1_idx))]
    args = [a1]
    if has_norm:
        in_specs.append(pl.BlockSpec((1, k1), lambda i, j: (0, 0)))
        args.append(g1.reshape(1, k1).astype(F32))
    in_specs.append(pl.BlockSpec((k1, bn), lambda i, j: (0, j)))
    args.append(w1)
    if has_a2:
        k2 = w2.shape[0]
        a2_idx = 0 if a2_cols is None else a2_cols[1]
        in_specs += [pl.BlockSpec((bm, k2), lambda i, j: (i, a2_idx)),
                     pl.BlockSpec((k2, bn), lambda i, j: (0, j))]
        args += [a2, w2]
    for ex in extras:
        if ex[0] == "tile":
            off = ex[2]
            in_specs.append(pl.BlockSpec((bm, bn), lambda i, j, off=off: (i, j + off)))
            args.append(ex[1])
        else:
            in_specs.append(pl.BlockSpec((1, bn), lambda i, j: (0, j)))
            args.append(ex[1].reshape(1, n).astype(F32))
    out_shape = [jax.ShapeDtypeStruct((m, n), out_dtype)]
    out_specs = [pl.BlockSpec((bm, bn), lambda i, j: (i, j))]
    if has_side:
        out_shape.append(jax.ShapeDtypeStruct((m, k1), side_dtype))
        out_specs.append(pl.BlockSpec((bm, k1), lambda i, j: (i, 0)))
    scratch = [pltpu.VMEM((bm, k1), BF16)] if has_norm else []
    res = pl.pallas_call(
        functools.partial(_fused_mm_body, has_norm=has_norm, has_a2=has_a2, n_extras=len(extras),
                          epilogue=epilogue, has_side=has_side),
        grid=(m // bm, n // bn), in_specs=in_specs, out_specs=out_specs, out_shape=out_shape,
        scratch_shapes=scratch, compiler_params=_cparams(2), name=name)(*args)
    return res if has_side else res[0]


def _latent_rows_body(c_ref, pe_ref, g_ref, lat_ref, ext_ref):
    x = c_ref[...]
    ms = jnp.mean(x * x, axis=-1, keepdims=True)
    lat = x * lax.rsqrt(ms + EPS) * g_ref[...]
    lat_ref[...] = lat
    kv = lat_ref.shape[1]
    ext_ref[:, :kv] = lat.astype(BF16)
    ext_ref[:, kv:] = pe_ref[...].astype(BF16)


def latent_rows(z, g, *, c_blk, pe_blk, kv_lora, bm):
    m = z.shape[0]
    return pl.pallas_call(
        _latent_rows_body, grid=(m // bm,),
        in_specs=[pl.BlockSpec((bm, kv_lora), lambda i: (i, c_blk)),
                  pl.BlockSpec((bm, 128), lambda i: (i, pe_blk)),
                  pl.BlockSpec((1, kv_lora), lambda i: (0, 0))],
        out_specs=[pl.BlockSpec((bm, kv_lora), lambda i: (i, 0)),
                   pl.BlockSpec((bm, kv_lora + 128), lambda i: (i, 0))],
        out_shape=[jax.ShapeDtypeStruct((m, kv_lora), F32),
                   jax.ShapeDtypeStruct((m, kv_lora + 128), BF16)],
        compiler_params=_cparams(1), name="latent_rows")(z, z, g.reshape(1, kv_lora).astype(F32))


def _head_mm_body(*refs, has_norm, scale):
    if has_norm:
        a_ref, g_ref, w_ref, gv_ref, ca_ref, sa_ref, out_ref, h_ref = refs

        @pl.when(pl.program_id(1) == 0)
        def _():
            x = a_ref[...].astype(F32)
            ms = jnp.mean(x * x, axis=-1, keepdims=True)
            h_ref[...] = (x * lax.rsqrt(ms + EPS) * g_ref[...]).astype(BF16)

        lhs = h_ref[...]
    else:
        a_ref, w_ref, gv_ref, ca_ref, sa_ref, out_ref = refs
        lhs = a_ref[...].astype(BF16)
    acc = jnp.dot(lhs, w_ref[...], preferred_element_type=F32)
    bm, bn = acc.shape
    ca = ca_ref[...]
    sa = sa_ref[...]
    rope_lane = lax.broadcasted_iota(jnp.int32, (bm, 128), 1) < QK_ROPE
    for h in range(bn // HEAD_PAD):
        c0 = h * HEAD_PAD
        z1 = acc[:, c0:c0 + 128]
        z2 = acc[:, c0 + 128:c0 + 256]
        z2r = jnp.where(rope_lane, z2, 0.0)
        ss = jnp.sum(z1 * z1, axis=-1, keepdims=True) + jnp.sum(z2r * z2r, axis=-1, keepdims=True)
        r = lax.rsqrt(ss * (1.0 / QK_DIM) + EPS) * scale
        out_ref[:, c0:c0 + 128] = (z1 * gv_ref[:, c0:c0 + 128] * r).astype(out_ref.dtype)
        rot = pltpu.roll(z2, 64, axis=1)
        o2 = (z2 * gv_ref[:, c0 + 128:c0 + 256] * ca + rot * sa) * r
        out_ref[:, c0 + 128:c0 + 256] = o2.astype(out_ref.dtype)


def head_mm(a, w, gv, ca, sa, *, scale, bm, bn, g=None, a_cols=None, name="head_mm"):
    m = a.shape[0]
    k, n = w.shape
    a_idx = 0 if a_cols is None else a_cols[1]
    n_tab = ca.shape[0] // bm
    has_norm = g is not None
    in_specs = [pl.BlockSpec((bm, k), lambda i, j: (i, a_idx))]
    args = [a]
    if has_norm:
        in_specs.append(pl.BlockSpec((1, k), lambda i, j: (0, 0)))
        args.append(g.reshape(1, k).astype(F32))
    in_specs += [pl.BlockSpec((k, bn), lambda i, j: (0, j)),
                 pl.BlockSpec((1, bn), lambda i, j: (0, j)),
                 pl.BlockSpec((bm, 128), lambda i, j: (i % n_tab, 0)),
                 pl.BlockSpec((bm, 128), lambda i, j: (i % n_tab, 0))]
    args += [w, gv.reshape(1, n).astype(F32), ca, sa]
    return pl.pallas_call(
        functools.partial(_head_mm_body, has_norm=has_norm, scale=scale),
        grid=(m // bm, n // bn), in_specs=in_specs,
        out_specs=pl.BlockSpec((bm, bn), lambda i, j: (i, j)),
        out_shape=jax.ShapeDtypeStruct((m, n), BF16),
        scratch_shapes=[pltpu.VMEM((bm, k), BF16)] if has_norm else [],
        compiler_params=_cparams(2), name=name)(*args)


def _flash_body(qt_ref, kt_ref, q_ref, k_ref, v_ref, o_ref, m_ref, l_ref, acc_ref, *, tq, tk):
    p = pl.program_id(2)
    qi = qt_ref[p]
    ki = kt_ref[p]

    @pl.when(ki == 0)
    def _():
        m_ref[...] = jnp.full(m_ref.shape, NEG_BIG, F32)
        l_ref[...] = jnp.zeros(l_ref.shape, F32)
        acc_ref[...] = jnp.zeros(acc_ref.shape, F32)

    s = lax.dot_general(q_ref[...], k_ref[...], _NT, preferred_element_type=F32)
    row = qi * tq + lax.broadcasted_iota(jnp.int32, (tq, tk), 0)
    col = ki * tk + lax.broadcasted_iota(jnp.int32, (tq, tk), 1)
    s = jnp.where(col <= row, s, NEG_BIG)
    m_old = m_ref[...]
    m_new = jnp.maximum(m_old, jnp.max(s, axis=-1, keepdims=True))
    alpha = jnp.exp(m_old - m_new)
    pr = jnp.exp(s - m_new)
    l_ref[...] = alpha * l_ref[...] + jnp.sum(pr, axis=-1, keepdims=True)
    acc_ref[...] = alpha * acc_ref[...] + jnp.dot(pr.astype(BF16), v_ref[...], preferred_element_type=F32)
    m_ref[...] = m_new

    @pl.when(ki == ((qi + 1) * tq - 1) // tk)
    def _():
        o_ref[...] = (acc_ref[...] / l_ref[...]).astype(o_ref.dtype)


def flash_attention(q, k, v, *, batch, seq, tq, tk):
    nq, nk = seq // tq, seq // tk
    pairs = [(qi, ki) for qi in range(nq) for ki in range(((qi + 1) * tq - 1) // tk + 1)]
    qt = jnp.asarray([p[0] for p in pairs], jnp.int32)
    kt = jnp.asarray([p[1] for p in pairs], jnp.int32)
    grid_spec = pltpu.PrefetchScalarGridSpec(
        num_scalar_prefetch=2, grid=(batch, N_HEADS, len(pairs)),
        in_specs=[pl.BlockSpec((tq, HEAD_PAD), lambda b, h, p, qt, kt: (b * nq + qt[p], h)),
                  pl.BlockSpec((tk, HEAD_PAD), lambda b, h, p, qt, kt: (b * nk + kt[p], h)),
                  pl.BlockSpec((tk, V_DIM), lambda b, h, p, qt, kt: (b * nk + kt[p], h))],
        out_specs=pl.BlockSpec((tq, V_DIM), lambda b, h, p, qt, kt: (b * nq + qt[p], h)),
        scratch_shapes=[pltpu.VMEM((tq, 1), F32), pltpu.VMEM((tq, 1), F32), pltpu.VMEM((tq, V_DIM), F32)])
    return pl.pallas_call(
        functools.partial(_flash_body, tq=tq, tk=tk), grid_spec=grid_spec,
        out_shape=jax.ShapeDtypeStruct((batch * seq, N_HEADS * V_DIM), BF16),
        compiler_params=_cparams(3), name="flash_attention")(qt, kt, q, k, v)


def _block_mm_body(x_ref, w_ref, s_ref, o_ref):
    acc = jnp.dot(x_ref[...].astype(BF16), w_ref[...], preferred_element_type=F32)
    o_ref[...] = (acc * s_ref[...]).astype(o_ref.dtype)


def block_mm(x, w, *, out_dtype, bm, col_scale=None, name="block_mm"):
    nh, k, n = w.shape
    if x.ndim == 2:
        m = x.shape[0]
        x_spec = pl.BlockSpec((bm, k), lambda i, h: (i, h))
    else:
        m = x.shape[1]
        x_spec = pl.BlockSpec((None, bm, k), lambda i, h: (h, i, 0))
    if col_scale is None:
        col_scale = jnp.ones((nh * n,), F32)
    return pl.pallas_call(
        _block_mm_body, grid=(m // bm, nh),
        in_specs=[x_spec, pl.BlockSpec((None, k, n), lambda i, h: (h, 0, 0)),
                  pl.BlockSpec((1, n), lambda i, h: (0, h))],
        out_specs=pl.BlockSpec((bm, n), lambda i, h: (i, h)),
        out_shape=jax.ShapeDtypeStruct((m, nh * n), out_dtype),
        compiler_params=_cparams(2), name=name)(x, w, col_scale.reshape(1, nh * n).astype(F32))


def _pool_diff_body(hist_ref, u_ref, d_ref, ext_ref, *, tb, start_pos):
    t = pl.program_id(1)

    @pl.when(t == 0)
    def _():
        ext_ref[0:POOL_HIST, :] = hist_ref[...]

    u = u_ref[...]
    ext_ref[POOL_HIST:, :] = u
    pos = start_pos + t * tb + lax.broadcasted_iota(jnp.int32, (tb, POOL_GROUP), 0) + 1
    for gi, w in enumerate(POOL_WINDOWS):
        c0 = gi * POOL_GROUP
        s = u[:, c0:c0 + POOL_GROUP]
        for j in range(1, w):
            s = s + ext_ref[POOL_HIST - j:POOL_HIST - j + tb, c0:c0 + POOL_GROUP]
        cnt = jnp.minimum(w, pos).astype(F32)
        d_ref[:, c0:c0 + POOL_GROUP] = (s / cnt - u[:, c0:c0 + POOL_GROUP]).astype(d_ref.dtype)
    ext_ref[0:POOL_HIST, :] = ext_ref[tb:tb + POOL_HIST, :]


def pool_diff(z, hist, *, u_blk, width, batch, seq, tb, start_pos):
    nt = seq // tb
    return pl.pallas_call(
        functools.partial(_pool_diff_body, tb=tb, start_pos=start_pos), grid=(batch, nt),
        in_specs=[pl.BlockSpec((None, POOL_HIST, width), lambda b, t: (b, 0, 0)),
                  pl.BlockSpec((tb, width), lambda b, t: (b * nt + t, u_blk))],
        out_specs=pl.BlockSpec((tb, width), lambda b, t: (b * nt + t, 0)),
        out_shape=jax.ShapeDtypeStruct((batch * seq, width), BF16),
        scratch_shapes=[pltpu.VMEM((tb + POOL_HIST, width), F32)],
        compiler_params=_cparams(2), name="pool_diff")(hist, z)


def _attend_tile(qx, wukT, c_bf, pe, cos2, sin2, valid, m_ref, l_ref, acc_ref):
    n = c_bf.shape[0]
    kv = c_bf.shape[1]
    kn_t = lax.dot_general(wukT, c_bf, _NT, preferred_element_type=F32)
    ssn = jnp.sum((kn_t * kn_t).reshape(N_HEADS, QK_NOPE, n), axis=1)
    ones = jnp.ones((8, QK_ROPE), BF16)
    sspe = lax.dot_general(ones, (pe * pe).astype(BF16), _NT, preferred_element_type=F32)[0:1]
    r = lax.rsqrt((ssn + sspe) * (1.0 / QK_DIM) + EPS)
    s = lax.dot_general(qx[:, :kv], c_bf, _NT, preferred_element_type=F32)
    s = s + lax.dot_general(qx[:, kv:kv + QK_ROPE], (pe * cos2).astype(BF16), _NT, preferred_element_type=F32)
    s = s + lax.dot_general(qx[:, kv + QK_ROPE:], (pe * sin2).astype(BF16), _NT, preferred_element_type=F32)
    rows = s.shape[0]
    nq = rows // N_HEADS
    s = (s.reshape(N_HEADS, nq, n) * r[:, None, :]).reshape(rows, n)
    if valid is not None:
        s = jnp.where(valid, s, NEG_BIG)
    m_old = m_ref[...]
    m_new = jnp.maximum(m_old, jnp.max(s, axis=-1, keepdims=True))
    alpha = jnp.exp(m_old - m_new)
    pr = jnp.exp(s - m_new)
    l_ref[...] = alpha * l_ref[...] + jnp.sum(pr, axis=-1, keepdims=True)
    acc_ref[...] = alpha * acc_ref[...] + jnp.dot(pr.astype(BF16), c_bf, preferred_element_type=F32)
    m_ref[...] = m_new


def _sample_attn_body(pt_ref, *refs, n_pg, n_tiles, dec_seq):
    qx_ref, wuk_ref = refs[0], refs[1]
    c_refs = refs[2:2 + n_pg]
    pe_refs = refs[2 + n_pg:2 + 2 * n_pg]
    cos_ref, sin_ref, cosn_ref, sinn_ref, cnew_ref, penew_ref, o_ref, m_ref, l_ref, acc_ref = refs[2 + 2 * n_pg:]
    j = pl.program_id(1)

    @pl.when(j == 0)
    def _():
        m_ref[...] = jnp.full(m_ref.shape, NEG_BIG, F32)
        l_ref[...] = jnp.zeros(l_ref.shape, F32)
        acc_ref[...] = jnp.zeros(acc_ref.shape, F32)

    qx = qx_ref[...]
    wuk = wuk_ref[...]

    @pl.when(j < n_tiles)
    def _():
        c_bf = jnp.concatenate([r[...].astype(BF16) for r in c_refs], axis=0)
        pe = jnp.concatenate([r[...] for r in pe_refs], axis=0)
        _attend_tile(qx, wuk, c_bf, pe, cos_ref[...], sin_ref[...], None, m_ref, l_ref, acc_ref)

    @pl.when(j == n_tiles)
    def _():
        rows = qx.shape[0]
        npad = 128
        kv = cnew_ref.shape[1]
        c_bf = jnp.concatenate([cnew_ref[...], jnp.zeros((npad - dec_seq, kv), F32)], axis=0).astype(BF16)
        pe = jnp.concatenate([penew_ref[...], jnp.zeros((npad - dec_seq, QK_ROPE), F32)], axis=0)
        qpos = lax.broadcasted_iota(jnp.int32, (rows, npad), 0) % dec_seq
        kpos = lax.broadcasted_iota(jnp.int32, (rows, npad), 1)
        _attend_tile(qx, wuk, c_bf, pe, cosn_ref[...], sinn_ref[...], kpos <= qpos, m_ref, l_ref, acc_ref)
        o_ref[...] = acc_ref[...] / l_ref[...]


def sample_attention(qx, wukT, cache_c, cache_pe, page_table, cos_k, sin_k, cos_n, sin_n, c_new, pe_new,
                     *, dec_seq, pages_per_tile):
    db, n_pages = page_table.shape
    page = cache_c.shape[1]
    kv = cache_c.shape[2]
    n_pg = pages_per_tile
    assert n_pages % n_pg == 0
    n_tiles = n_pages // n_pg
    tk = n_pg * page
    rows = N_HEADS * dec_seq
    last = n_tiles - 1

    def page_map(i):
        return lambda b, j, pt: (pt[b, jnp.minimum(j, last) * n_pg + i], 0, 0)

    in_specs = [pl.BlockSpec((rows, qx.shape[1]), lambda b, j, pt: (b, 0)),
                pl.BlockSpec(wukT.shape, lambda b, j, pt: (0, 0))]
    in_specs += [pl.BlockSpec((None, page, kv), page_map(i)) for i in range(n_pg)]
    in_specs += [pl.BlockSpec((None, page, QK_ROPE), page_map(i)) for i in range(n_pg)]
    in_specs += [pl.BlockSpec((tk, QK_ROPE), lambda b, j, pt: (jnp.minimum(j, last), 0)),
                 pl.BlockSpec((tk, QK_ROPE), lambda b, j, pt: (jnp.minimum(j, last), 0)),
                 pl.BlockSpec(cos_n.shape, lambda b, j, pt: (0, 0)),
                 pl.BlockSpec(sin_n.shape, lambda b, j, pt: (0, 0)),
                 pl.BlockSpec((dec_seq, kv), lambda b, j, pt: (b, 0)),
                 pl.BlockSpec((dec_seq, QK_ROPE), lambda b, j, pt: (b, 0))]
    grid_spec = pltpu.PrefetchScalarGridSpec(
        num_scalar_prefetch=1, grid=(db, n_tiles + 1), in_specs=in_specs,
        out_specs=pl.BlockSpec((rows, kv), lambda b, j, pt: (b, 0)),
        scratch_shapes=[pltpu.VMEM((rows, 1), F32), pltpu.VMEM((rows, 1), F32), pltpu.VMEM((rows, kv), F32)])
    return pl.pallas_call(
        functools.partial(_sample_attn_body, n_pg=n_pg, n_tiles=n_tiles, dec_seq=dec_seq),
        grid_spec=grid_spec, out_shape=jax.ShapeDtypeStruct((db * rows, kv), F32),
        compiler_params=_cparams(2), name="sample_attention")(
            page_table, qx, wukT, *([cache_c] * n_pg), *([cache_pe] * n_pg),
            cos_k, sin_k, cos_n, sin_n, c_new, pe_new)


def _rope_tables(pos):
    half = QK_ROPE // 2
    freq = ROPE_THETA ** (-jnp.arange(half, dtype=F32) / half)
    ang = pos.astype(F32)[:, None] * freq[None, :]
    cos, sin = jnp.cos(ang), jnp.sin(ang)
    return jnp.concatenate([cos, cos], axis=-1), jnp.concatenate([sin, sin], axis=-1)


def _pad_lanes(t):
    return jnp.concatenate([t, jnp.zeros_like(t)], axis=-1)


def _rot_cols(w_rope, g_rope):
    half = QK_ROPE // 2
    return jnp.concatenate([-w_rope[..., half:] * g_rope[half:], w_rope[..., :half] * g_rope[:half]], axis=-1)


def _head_gain(g):
    return jnp.tile(jnp.concatenate([g, jnp.zeros((HEAD_PAD - QK_DIM,), F32)]), N_HEADS)


def _layer(x, p, hist, cache, wts, *, batch, seq, start_pos):
    t = x.shape[0]
    d_model = x.shape[1]
    q_lora = wts["q_norm_g"].shape[0]
    kv_lora = wts["kv_norm_g"].shape[0]
    pool_w = wts["pool_scale"].shape[0]
    bm = min(1024, t)
    ident = lambda acc: acc

    z = fused_mm(x, wts["w_in"], g1=wts["norm1_g"], epilogue=ident, out_dtype=F32, bm=bm, bn=1280, name="in_proj")
    off_u = q_lora + kv_lora
    off_ga = off_u + pool_w
    off_gb = off_ga + d_model
    off_pe = off_gb + d_model
    pe_raw = z[:, off_pe:off_pe + QK_ROPE]
    c_lat, ckv = latent_rows(z, wts["kv_norm_g"], c_blk=q_lora // kv_lora, pe_blk=off_pe // 128,
                             kv_lora=kv_lora, bm=min(512, t))

    pos = start_pos + jnp.arange(seq, dtype=jnp.int32)
    cos2, sin2 = _rope_tables(pos)
    n_rep = 1 if cache is None else batch
    ca, sa = jnp.tile(_pad_lanes(cos2), (n_rep, 1)), jnp.tile(_pad_lanes(sin2), (n_rep, 1))
    hb = min(512, t)
    q = head_mm(z, wts["w_q"], wts["q_gain"], ca, sa, scale=QK_DIM ** -0.5, bm=hb, bn=1024,
                g=wts["q_norm_g"], a_cols=(q_lora, 0), name="q_proj")

    if cache is None:
        k = head_mm(ckv, wts["w_k"], wts["k_gain"], ca, sa, scale=1.0, bm=hb, bn=1024, name="k_proj")
        v = fused_mm(ckv, wts["w_v"], a1_cols=(kv_lora, 0), epilogue=ident, out_dtype=BF16, bm=bm, bn=1024,
                     name="v_proj")
        o = flash_attention(q, k, v, batch=batch, seq=seq, tq=min(1024, seq), tk=min(512, seq))
    else:
        cache_c, cache_pe, page_table = cache
        past = page_table.shape[1] * cache_c.shape[1]
        qx = block_mm(q, wts["w_absorb"], out_dtype=BF16, bm=bm, name="q_absorb")
        xw = kv_lora + 2 * QK_ROPE
        qx = qx.reshape(batch, seq, N_HEADS, xw).transpose(0, 2, 1, 3).reshape(t * N_HEADS, xw)
        cos_k, sin_k = _rope_tables(jnp.arange(past, dtype=jnp.int32))
        pad = jnp.zeros((128 - seq, QK_ROPE), F32)
        cos_n, sin_n = jnp.concatenate([cos2, pad]), jnp.concatenate([sin2, pad])
        o_lat = sample_attention(qx, wts["w_ukT"], cache_c, cache_pe, page_table, cos_k, sin_k, cos_n, sin_n,
                                 c_lat, pe_raw, dec_seq=seq, pages_per_tile=min(8, page_table.shape[1]))
        o_lat = o_lat.reshape(batch, N_HEADS, seq, kv_lora).transpose(1, 0, 2, 3).reshape(N_HEADS, t, kv_lora)
        o = block_mm(o_lat, wts["w_uv_heads"], out_dtype=BF16, bm=bm, name="o_up")

    d = pool_diff(z, hist, u_blk=off_u // pool_w, width=pool_w, batch=batch, seq=seq,
                  tb=min(512, seq), start_pos=start_pos)
    y = block_mm(d, wts["w_pool"], out_dtype=BF16, bm=bm, col_scale=wts["pool_scale"], name="pool_map")

    def merge_epi(a, b, ga, gb):
        return jax.nn.sigmoid(ga) * a + jax.nn.sigmoid(gb) * b

    mm = fused_mm(o, wts["w_a"], a2=y, w2=wts["w_b"], epilogue=merge_epi, out_dtype=BF16, bm=bm, bn=1024,
                  extras=(("tile", z, off_ga // 1024), ("tile", z, off_gb // 1024)), name="merge")
    x1 = fused_mm(mm, wts["w_out"], epilogue=lambda acc, xr: xr + acc, out_dtype=F32, bm=bm, bn=1024,
                  extras=(("tile", x, 0),), name="out_proj")

    q_peer, hn = fused_mm(x1, wts["w_peer_q"], g1=wts["norm2_g"], epilogue=ident, out_dtype=F32, bm=bm, bn=1024,
                          side_dtype=F32, name="peer_q")
    x2 = x1 + _peer_tail(q_peer, hn, wts)

    def ple_epi(gate, emb, xr):
        return xr + emb * jax.nn.sigmoid(gate)

    x3 = fused_mm(x2, wts["w_ple_gate"], g1=wts["norm3_g"], a2=p, w2=wts["w_ple"], epilogue=ple_epi,
                  out_dtype=F32, bm=bm, bn=1024, extras=(("tile", x2, 0),), name="ple")
    u_pool = z[:, off_u:off_u + pool_w].reshape(batch, seq, pool_w)
    return x3, c_lat, pe_raw, u_pool


def _peer_tail(q_peer, hn, wts):
    n = q_peer.shape[0]
    q = q_peer.reshape(n, PEER_HEADS, 2, PEER_HALF)
    s = jnp.einsum('thpc,hpkc->thpk', q, wts["peer_subkeys"])
    s_top, i_top = lax.top_k(s, PEER_TOPK)
    cand = s_top[:, :, 0, :, None] + s_top[:, :, 1, None, :]
    cand_idx = i_top[:, :, 0, :, None] * N_KEYS + i_top[:, :, 1, None, :]
    best, pos = lax.top_k(cand.reshape(n, PEER_HEADS, PEER_TOPK * PEER_TOPK), PEER_TOPK)
    idx = jnp.take_along_axis(cand_idx.reshape(n, PEER_HEADS, PEER_TOPK * PEER_TOPK), pos, axis=-1)
    g = jax.nn.softmax(best, axis=-1)

    def block(args):
        xb, ib, gb = args
        act = jax.nn.gelu(jnp.einsum('thkd,td->thk', wts["peer_u"][ib], xb), approximate=False)
        return jnp.einsum('thk,thkd->td', gb * act, wts["peer_v"][ib])

    nb = n // PEER_BLOCK
    y = lax.map(block, (hn.reshape(nb, PEER_BLOCK, -1), idx.reshape(nb, PEER_BLOCK, PEER_HEADS, PEER_TOPK),
                        g.reshape(nb, PEER_BLOCK, PEER_HEADS, PEER_TOPK)))
    return y.reshape(n, -1)


def _prep_weights(i, norm1_g, w_in, q_norm_g, w_q_up, kv_norm_g, w_kv_up, q_head_g, k_head_g, w_pool_map,
                  pool_scale, w_a_proj, w_b_proj, w_out, norm2_g, w_peer_q, peer_subkeys, peer_u, peer_v,
                  norm3_g, w_ple, w_ple_gate):
    d_model = w_in.shape[1]
    q_lora, kv_lora = q_norm_g.shape[1], kv_norm_g.shape[1]
    pool_w = pool_scale.shape[1]
    s0, s1, s2, s3 = q_lora, q_lora + kv_lora, q_lora + kv_lora + QK_ROPE, q_lora + kv_lora + QK_ROPE + pool_w
    wi = w_in[i]
    n_used = wi.shape[1]
    n_pad = -(-n_used // 1280) * 1280
    w_in_p = jnp.concatenate([wi[:, :s1], wi[:, s2:], wi[:, s1:s2], jnp.zeros((d_model, n_pad - n_used), F32)], axis=1)
    qg, kg = q_head_g[i], k_head_g[i]
    wq = w_q_up[i].reshape(q_lora, N_HEADS, QK_DIM)
    wq_r = wq[..., QK_NOPE:]
    w_q = jnp.concatenate([wq[..., :QK_NOPE], wq_r, _rot_cols(wq_r, qg[QK_NOPE:])], axis=-1).reshape(q_lora, -1)
    wkv = w_kv_up[i].reshape(kv_lora, N_HEADS, QK_NOPE + V_DIM)
    w_uk, w_uv = wkv[..., :QK_NOPE], wkv[..., QK_NOPE:]
    eye = jnp.eye(QK_ROPE, dtype=F32)
    pe_cols = jnp.concatenate([jnp.zeros((QK_ROPE, QK_NOPE), F32), eye, _rot_cols(eye, kg[QK_NOPE:])], axis=-1)
    top = jnp.concatenate([w_uk, jnp.zeros((kv_lora, N_HEADS, HEAD_PAD - QK_NOPE), F32)], axis=-1)
    mid = jnp.broadcast_to(pe_cols[:, None, :], (QK_ROPE, N_HEADS, HEAD_PAD))
    bot = jnp.zeros((128 - QK_ROPE, N_HEADS, HEAD_PAD), F32)
    w_k = jnp.concatenate([top, mid, bot], axis=0).reshape(kv_lora + 128, -1)
    g_kn, g_kr = kg[:QK_NOPE], kg[QK_NOPE:]
    half = QK_ROPE // 2
    absorb = jnp.transpose(w_uk, (1, 2, 0)) * g_kn[None, :, None]
    swap = jnp.concatenate([jnp.concatenate([jnp.zeros((half, half), F32), -jnp.eye(half, dtype=F32)], axis=1),
                            jnp.concatenate([jnp.eye(half, dtype=F32), jnp.zeros((half, half), F32)], axis=1)], axis=0)
    rope_map = jnp.concatenate([eye * g_kr[None, :], swap * g_kr[None, :]], axis=1)
    w_absorb = jnp.zeros((N_HEADS, HEAD_PAD, kv_lora + 2 * QK_ROPE), F32)
    w_absorb = w_absorb.at[:, :QK_NOPE, :kv_lora].set(absorb)
    w_absorb = w_absorb.at[:, QK_NOPE:QK_DIM, kv_lora:].set(jnp.broadcast_to(rope_map, (N_HEADS,) + rope_map.shape))
    return {
        "norm1_g": norm1_g[i], "w_in": w_in_p.astype(BF16), "q_norm_g": q_norm_g[i], "kv_norm_g": kv_norm_g[i],
        "w_q": w_q.astype(BF16), "q_gain": _head_gain(qg), "w_k": w_k.astype(BF16), "k_gain": _head_gain(kg),
        "w_v": w_uv.reshape(kv_lora, -1).astype(BF16),
        "w_ukT": jnp.transpose(w_uk, (1, 2, 0)).reshape(N_HEADS * QK_NOPE, kv_lora).astype(BF16),
        "w_uv_heads": jnp.transpose(w_uv, (1, 0, 2)).astype(BF16), "w_absorb": w_absorb.astype(BF16),
        "w_pool": w_pool_map[i].astype(BF16), "pool_scale": pool_scale[i],
        "w_a": w_a_proj[i].astype(BF16), "w_b": w_b_proj[i].astype(BF16), "w_out": w_out[i].astype(BF16),
        "norm2_g": norm2_g[i], "w_peer_q": w_peer_q[i].astype(BF16), "peer_subkeys": peer_subkeys[i],
        "peer_u": peer_u[i], "peer_v": peer_v[i], "norm3_g": norm3_g[i],
        "w_ple": w_ple[i].astype(BF16), "w_ple_gate": w_ple_gate[i].astype(BF16),
    }


def kernel(x_prompt, x_sample, p_prompt, p_sample, cache_kv_latent, cache_k_rope, state_pool, page_table, norm1_g, w_in, q_norm_g, w_q_up, kv_norm_g, w_kv_up, q_head_g, k_head_g, w_pool_map, pool_scale, w_a_proj, w_b_proj, w_out, norm2_g, w_peer_q, peer_subkeys, peer_u, peer_v, norm3_g, w_ple, w_ple_gate):
    b, s, d_model = x_prompt.shape
    db, ds, _ = x_sample.shape
    depth = w_in.shape[0]
    pool_w = pool_scale.shape[1]
    pool_pad = state_pool.shape[2]
    past_len = page_table.shape[1] * cache_kv_latent.shape[2]
    xp = x_prompt.reshape(b * s, d_model)
    xs = x_sample.reshape(db * ds, d_model)
    outs = [[] for _ in range(6)]
    for i in range(depth):
        wts = _prep_weights(i, norm1_g, w_in, q_norm_g, w_q_up, kv_norm_g, w_kv_up, q_head_g, k_head_g, w_pool_map,
                            pool_scale, w_a_proj, w_b_proj, w_out, norm2_g, w_peer_q, peer_subkeys, peer_u, peer_v,
                            norm3_g, w_ple, w_ple_gate)
        hist_p = jnp.zeros((b, POOL_HIST, pool_w), F32)
        xp, lat, pe, u = _layer(xp, p_prompt[i].reshape(b * s, -1), hist_p, None, wts, batch=b, seq=s, start_pos=0)
        outs[0].append(lat.reshape(b, s, -1))
        outs[1].append(pe.reshape(b, s, -1))
        outs[2].append(jnp.concatenate([hist_p[:, POOL_HIST - pool_pad:], u], axis=1)[:, -pool_pad:])
        hist_s = jnp.concatenate([jnp.zeros((db, POOL_HIST - pool_pad, pool_w), F32), state_pool[i]], axis=1)
        cache = (cache_kv_latent[i], cache_k_rope[i], page_table)
        xs, lat, pe, u = _layer(xs, p_sample[i].reshape(db * ds, -1), hist_s, cache, wts, batch=db, seq=ds,
                                start_pos=past_len)
        outs[3].append(lat.reshape(db, ds, -1))
        outs[4].append(pe.reshape(db, ds, -1))
        outs[5].append(jnp.concatenate([state_pool[i], u], axis=1)[:, -pool_pad:])
    return (xp.reshape(b, s, d_model), xs.reshape(db, ds, d_model)) + tuple(jnp.stack(o) for o in outs)
```

```python
import functools

import jax
import jax.numpy as jnp
import numpy as np
from jax import lax
from jax.experimental import pallas as pl
from jax.experimental.pallas import tpu as pltpu

F32 = jnp.float32
BF16 = jnp.bfloat16

N_HEADS = 16
QK_NOPE = 128
QK_ROPE = 64
QK_DIM = QK_NOPE + QK_ROPE
V_DIM = 128
HEAD_PAD = 256
ROPE_THETA = 10000.0
POOL_WINDOWS = (2, 4, 8, 16)
POOL_GROUP = 256
POOL_HIST = 16
PEER_HEADS = 8
N_KEYS = 128
PEER_TOPK = 16
PEER_HALF = 128
PEER_BLOCK = 128
EPS = 1e-6
NEG_BIG = -1e30

VMEM_LIMIT_BYTES = 56 * 1024 * 1024
_NT = (((1,), (1,)), ((), ()))


def _cparams(n_axes):
    return pltpu.CompilerParams(dimension_semantics=("arbitrary",) * n_axes,
                                vmem_limit_bytes=VMEM_LIMIT_BYTES)


def _fused_mm_body(*refs, has_norm, has_a2, n_extras, epilogue, has_side):
    it = iter(refs)
    a1_ref = next(it)
    g1_ref = next(it) if has_norm else None
    w1_ref = next(it)
    a2_ref = next(it) if has_a2 else None
    w2_ref = next(it) if has_a2 else None
    extra_refs = [next(it) for _ in range(n_extras)]
    out_ref = next(it)
    side_ref = next(it) if has_side else None
    h_ref = next(it) if has_norm else None

    if has_norm:
        @pl.when(pl.program_id(1) == 0)
        def _():
            x = a1_ref[...].astype(F32)
            ms = jnp.mean(x * x, axis=-1, keepdims=True)
            h = x * lax.rsqrt(ms + EPS) * g1_ref[...]
            h_ref[...] = h.astype(BF16)
            if has_side:
                side_ref[...] = h.astype(side_ref.dtype)

        lhs = h_ref[...]
    else:
        lhs = a1_ref[...].astype(BF16)
    accs = [jnp.dot(lhs, w1_ref[...], preferred_element_type=F32)]
    if has_a2:
        accs.append(jnp.dot(a2_ref[...].astype(BF16), w2_ref[...], preferred_element_type=F32))
    out_ref[...] = epilogue(*accs, *[r[...] for r in extra_refs]).astype(out_ref.dtype)


def fused_mm(a1, w1, *, epilogue, out_dtype, bm, bn, g1=None, a1_cols=None, a2=None, w2=None,
             a2_cols=None, extras=(), side_dtype=None, name="fused_mm"):
    m = a1.shape[0]
    k1, n = w1.shape
    assert m % bm == 0 and n % bn == 0
    a1_idx = 0 if a1_cols is None else a1_cols[1]
    assert (a1.shape[1] if a1_cols is None else a1_cols[0]) == k1
    has_norm, has_a2, has_side = g1 is not None, a2 is not None, side_dtype is not None
    in_specs = [pl.BlockSpec((bm, k1), lambda i, j: (i, a1_idx))]
    args = [a1]
    if has_norm:
        in_specs.append(pl.BlockSpec((1, k1), lambda i, j: (0, 0)))
        args.append(g1.reshape(1, k1).astype(F32))
    in_specs.append(pl.BlockSpec((k1, bn), lambda i, j: (0, j)))
    args.append(w1)
    if has_a2:
        k2 = w2.shape[0]
        a2_idx = 0 if a2_cols is None else a2_cols[1]
        in_specs += [pl.BlockSpec((bm, k2), lambda i, j: (i, a2_idx)),
                     pl.BlockSpec((k2, bn), lambda i, j: (0, j))]
        args += [a2, w2]
    for ex in extras:
        if ex[0] == "tile":
            off = ex[2]
            in_specs.append(pl.BlockSpec((bm, bn), lambda i, j, off=off: (i, j + off)))
            args.append(ex[1])
        else:
            in_specs.append(pl.BlockSpec((1, bn), lambda i, j: (0, j)))
            args.append(ex[1].reshape(1, n).astype(F32))
    out_shape = [jax.ShapeDtypeStruct((m, n), out_dtype)]
    out_specs = [pl.BlockSpec((bm, bn), lambda i, j: (i, j))]
    if has_side:
        out_shape.append(jax.ShapeDtypeStruct((m, k1), side_dtype))
        out_specs.append(pl.BlockSpec((bm, k1), lambda i, j: (i, 0)))
    scratch = [pltpu.VMEM((bm, k1), BF16)] if has_norm else []
    res = pl.pallas_call(
        functools.partial(_fused_mm_body, has_norm=has_norm, has_a2=has_a2, n_extras=len(extras),
                          epilogue=epilogue, has_side=has_side),
        grid=(m // bm, n // bn), in_specs=in_specs, out_specs=out_specs, out_shape=out_shape,
        scratch_shapes=scratch, compiler_params=_cparams(2), name=name)(*args)
    return res if has_side else res[0]


def _latent_rows_body(c_ref, pe_ref, g_ref, lat_ref, ext_ref):
    x = c_ref[...]
    ms = jnp.mean(x * x, axis=-1, keepdims=True)
    lat = x * lax.rsqrt(ms + EPS) * g_ref[...]
    lat_ref[...] = lat
    kv = lat_ref.shape[1]
    ext_ref[:, :kv] = lat.astype(BF16)
    ext_ref[:, kv:] = pe_ref[...].astype(BF16)


def latent_rows(z, g, *, c_blk, pe_blk, kv_lora, bm):
    m = z.shape[0]
    return pl.pallas_call(
        _latent_rows_body, grid=(m // bm,),
        in_specs=[pl.BlockSpec((bm, kv_lora), lambda i: (i, c_blk)),
                  pl.BlockSpec((bm, 128), lambda i: (i, pe_blk)),
                  pl.BlockSpec((1, kv_lora), lambda i: (0, 0))],
        out_specs=[pl.BlockSpec((bm, kv_lora), lambda i: (i, 0)),
                   pl.BlockSpec((bm, kv_lora + 128), lambda i: (i, 0))],
        out_shape=[jax.ShapeDtypeStruct((m, kv_lora), F32),
                   jax.ShapeDtypeStruct((m, kv_lora + 128), BF16)],
        compiler_params=_cparams(1), name="latent_rows")(z, z, g.reshape(1, kv_lora).astype(F32))


def _head_mm_body(*refs, has_norm, scale):
    if has_norm:
        a_ref, g_ref, w_ref, gv_ref, ca_ref, sa_ref, out_ref, h_ref = refs

        @pl.when(pl.program_id(1) == 0)
        def _():
            x = a_ref[...].astype(F32)
            ms = jnp.mean(x * x, axis=-1, keepdims=True)
            h_ref[...] = (x * lax.rsqrt(ms + EPS) * g_ref[...]).astype(BF16)

        lhs = h_ref[...]
    else:
        a_ref, w_ref, gv_ref, ca_ref, sa_ref, out_ref = refs
        lhs = a_ref[...].astype(BF16)
    acc = jnp.dot(lhs, w_ref[...], preferred_element_type=F32)
    bm, bn = acc.shape
    ca = ca_ref[...]
    sa = sa_ref[...]
    rope_lane = lax.broadcasted_iota(jnp.int32, (bm, 128), 1) < QK_ROPE
    for h in range(bn // HEAD_PAD):
        c0 = h * HEAD_PAD
        z1 = acc[:, c0:c0 + 128]
        z2 = acc[:, c0 + 128:c0 + 256]
        z2r = jnp.where(rope_lane, z2, 0.0)
        ss = jnp.sum(z1 * z1, axis=-1, keepdims=True) + jnp.sum(z2r * z2r, axis=-1, keepdims=True)
        r = lax.rsqrt(ss * (1.0 / QK_DIM) + EPS) * scale
        out_ref[:, c0:c0 + 128] = (z1 * gv_ref[:, c0:c0 + 128] * r).astype(out_ref.dtype)
        rot = pltpu.roll(z2, 64, axis=1)
        o2 = (z2 * gv_ref[:, c0 + 128:c0 + 256] * ca + rot * sa) * r
        out_ref[:, c0 + 128:c0 + 256] = o2.astype(out_ref.dtype)


def head_mm(a, w, gv, ca, sa, *, scale, bm, bn, g=None, a_cols=None, name="head_mm"):
    m = a.shape[0]
    k, n = w.shape
    a_idx = 0 if a_cols is None else a_cols[1]
    n_tab = ca.shape[0] // bm
    has_norm = g is not None
    in_specs = [pl.BlockSpec((bm, k), lambda i, j: (i, a_idx))]
    args = [a]
    if has_norm:
        in_specs.append(pl.BlockSpec((1, k), lambda i, j: (0, 0)))
        args.append(g.reshape(1, k).astype(F32))
    in_specs += [pl.BlockSpec((k, bn), lambda i, j: (0, j)),
                 pl.BlockSpec((1, bn), lambda i, j: (0, j)),
                 pl.BlockSpec((bm, 128), lambda i, j: (i % n_tab, 0)),
                 pl.BlockSpec((bm, 128), lambda i, j: (i % n_tab, 0))]
    args += [w, gv.reshape(1, n).astype(F32), ca, sa]
    return pl.pallas_call(
        functools.partial(_head_mm_body, has_norm=has_norm, scale=scale),
        grid=(m // bm, n // bn), in_specs=in_specs,
        out_specs=pl.BlockSpec((bm, bn), lambda i, j: (i, j)),
        out_shape=jax.ShapeDtypeStruct((m, n), BF16),
        scratch_shapes=[pltpu.VMEM((bm, k), BF16)] if has_norm else [],
        compiler_params=_cparams(2), name=name)(*args)


def _flash_body(qt_ref, kt_ref, q_ref, k_ref, v_ref, o_ref, m_ref, l_ref, acc_ref, *, tq, tk):
    p = pl.program_id(2)
    qi = qt_ref[p]
    ki = kt_ref[p]

    @pl.when(ki == 0)
    def _():
        m_ref[...] = jnp.full(m_ref.shape, NEG_BIG, F32)
        l_ref[...] = jnp.zeros(l_ref.shape, F32)
        acc_ref[...] = jnp.zeros(acc_ref.shape, F32)

    s = lax.dot_general(q_ref[...], k_ref[...], _NT, preferred_element_type=F32)
    row = qi * tq + lax.broadcasted_iota(jnp.int32, (tq, tk), 0)
    col = ki * tk + lax.broadcasted_iota(jnp.int32, (tq, tk), 1)
    s = jnp.where(col <= row, s, NEG_BIG)
    m_old = m_ref[...]
    m_new = jnp.maximum(m_old, jnp.max(s, axis=-1, keepdims=True))
    alpha = jnp.exp(m_old - m_new)
    pr = jnp.exp(s - m_new)
    l_ref[...] = alpha * l_ref[...] + jnp.sum(pr, axis=-1, keepdims=True)
    acc_ref[...] = alpha * acc_ref[...] + jnp.dot(pr.astype(BF16), v_ref[...], preferred_element_type=F32)
    m_ref[...] = m_new

    @pl.when(ki == ((qi + 1) * tq - 1) // tk)
    def _():
        o_ref[...] = (acc_ref[...] / l_ref[...]).astype(o_ref.dtype)


def flash_attention(q, k, v, *, batch, seq, tq, tk):
    nq, nk = seq // tq, seq // tk
    pairs = [(qi, ki) for qi in range(nq) for ki in range(((qi + 1) * tq - 1) // tk + 1)]
    qt = jnp.asarray([p[0] for p in pairs], jnp.int32)
    kt = jnp.asarray([p[1] for p in pairs], jnp.int32)
    grid_spec = pltpu.PrefetchScalarGridSpec(
        num_scalar_prefetch=2, grid=(batch, N_HEADS, len(pairs)),
        in_specs=[pl.BlockSpec((tq, HEAD_PAD), lambda b, h, p, qt, kt: (b * nq + qt[p], h)),
                  pl.BlockSpec((tk, HEAD_PAD), lambda b, h, p, qt, kt: (b * nk + kt[p], h)),
                  pl.BlockSpec((tk, V_DIM), lambda b, h, p, qt, kt: (b * nk + kt[p], h))],
        out_specs=pl.BlockSpec((tq, V_DIM), lambda b, h, p, qt, kt: (b * nq + qt[p], h)),
        scratch_shapes=[pltpu.VMEM((tq, 1), F32), pltpu.VMEM((tq, 1), F32), pltpu.VMEM((tq, V_DIM), F32)])
    return pl.pallas_call(
        functools.partial(_flash_body, tq=tq, tk=tk), grid_spec=grid_spec,
        out_shape=jax.ShapeDtypeStruct((batch * seq, N_HEADS * V_DIM), BF16),
        compiler_params=_cparams(3), name="flash_attention")(qt, kt, q, k, v)


def _block_mm_body(x_ref, w_ref, s_ref, o_ref):
    acc = jnp.dot(x_ref[...].astype(BF16), w_ref[...], preferred_element_type=F32)
    o_ref[...] = (acc * s_ref[...]).astype(o_ref.dtype)


def block_mm(x, w, *, out_dtype, bm, col_scale=None, name="block_mm"):
    nh, k, n = w.shape
    if x.ndim == 2:
        m = x.shape[0]
        x_spec = pl.BlockSpec((bm, k), lambda i, h: (i, h))
    else:
        m = x.shape[1]
        x_spec = pl.BlockSpec((None, bm, k), lambda i, h: (h, i, 0))
    if col_scale is None:
        col_scale = jnp.ones((nh * n,), F32)
    return pl.pallas_call(
        _block_mm_body, grid=(m // bm, nh),
        in_specs=[x_spec, pl.BlockSpec((None, k, n), lambda i, h: (h, 0, 0)),
                  pl.BlockSpec((1, n), lambda i, h: (0, h))],
        out_specs=pl.BlockSpec((bm, n), lambda i, h: (i, h)),
        out_shape=jax.ShapeDtypeStruct((m, nh * n), out_dtype),
        compiler_params=_cparams(2), name=name)(x, w, col_scale.reshape(1, nh * n).astype(F32))


def _pool_diff_body(hist_ref, u_ref, d_ref, ext_ref, *, tb, start_pos):
    t = pl.program_id(1)

    @pl.when(t == 0)
    def _():
        ext_ref[0:POOL_HIST, :] = hist_ref[...]

    u = u_ref[...]
    ext_ref[POOL_HIST:, :] = u
    pos = start_pos + t * tb + lax.broadcasted_iota(jnp.int32, (tb, POOL_GROUP), 0) + 1
    for gi, w in enumerate(POOL_WINDOWS):
        c0 = gi * POOL_GROUP
        s = u[:, c0:c0 + POOL_GROUP]
        for j in range(1, w):
            s = s + ext_ref[POOL_HIST - j:POOL_HIST - j + tb, c0:c0 + POOL_GROUP]
        cnt = jnp.minimum(w, pos).astype(F32)
        d_ref[:, c0:c0 + POOL_GROUP] = (s / cnt - u[:, c0:c0 + POOL_GROUP]).astype(d_ref.dtype)
    ext_ref[0:POOL_HIST, :] = ext_ref[tb:tb + POOL_HIST, :]


def pool_diff(z, hist, *, u_blk, width, batch, seq, tb, start_pos):
    nt = seq // tb
    return pl.pallas_call(
        functools.partial(_pool_diff_body, tb=tb, start_pos=start_pos), grid=(batch, nt),
        in_specs=[pl.BlockSpec((None, POOL_HIST, width), lambda b, t: (b, 0, 0)),
                  pl.BlockSpec((tb, width), lambda b, t: (b * nt + t, u_blk))],
        out_specs=pl.BlockSpec((tb, width), lambda b, t: (b * nt + t, 0)),
        out_shape=jax.ShapeDtypeStruct((batch * seq, width), BF16),
        scratch_shapes=[pltpu.VMEM((tb + POOL_HIST, width), F32)],
        compiler_params=_cparams(2), name="pool_diff")(hist, z)


def _attend_tile(qx, wukT, c_bf, pe, cos2, sin2, valid, m_ref, l_ref, acc_ref):
    n = c_bf.shape[0]
    kv = c_bf.shape[1]
    kn_t = lax.dot_general(wukT, c_bf, _NT, preferred_element_type=F32)
    ssn = jnp.sum((kn_t * kn_t).reshape(N_HEADS, QK_NOPE, n), axis=1)
    ones = jnp.ones((8, QK_ROPE), BF16)
    sspe = lax.dot_general(ones, (pe * pe).astype(BF16), _NT, preferred_element_type=F32)[0:1]
    r = lax.rsqrt((ssn + sspe) * (1.0 / QK_DIM) + EPS)
    s = lax.dot_general(qx[:, :kv], c_bf, _NT, preferred_element_type=F32)
    s = s + lax.dot_general(qx[:, kv:kv + QK_ROPE], (pe * cos2).astype(BF16), _NT, preferred_element_type=F32)
    s = s + lax.dot_general(qx[:, kv + QK_ROPE:], (pe * sin2).astype(BF16), _NT, preferred_element_type=F32)
    rows = s.shape[0]
    nq = rows // N_HEADS
    s = (s.reshape(N_HEADS, nq, n) * r[:, None, :]).reshape(rows, n)
    if valid is not None:
        s = jnp.where(valid, s, NEG_BIG)
    m_old = m_ref[...]
    m_new = jnp.maximum(m_old, jnp.max(s, axis=-1, keepdims=True))
    alpha = jnp.exp(m_old - m_new)
    pr = jnp.exp(s - m_new)
    l_ref[...] = alpha * l_ref[...] + jnp.sum(pr, axis=-1, keepdims=True)
    acc_ref[...] = alpha * acc_ref[...] + jnp.dot(pr.astype(BF16), c_bf, preferred_element_type=F32)
    m_ref[...] = m_new


def _sample_attn_body(pt_ref, *refs, n_pg, n_tiles, dec_seq):
    qx_ref, wuk_ref = refs[0], refs[1]
    c_refs = refs[2:2 + n_pg]
    pe_refs = refs[2 + n_pg:2 + 2 * n_pg]
    cos_ref, sin_ref, cosn_ref, sinn_ref, cnew_ref, penew_ref, o_ref, m_ref, l_ref, acc_ref = refs[2 + 2 * n_pg:]
    j = pl.program_id(1)

    @pl.when(j == 0)
    def _():
        m_ref[...] = jnp.full(m_ref.shape, NEG_BIG, F32)
        l_ref[...] = jnp.zeros(l_ref.shape, F32)
        acc_ref[...] = jnp.zeros(acc_ref.shape, F32)

    qx = qx_ref[...]
    wuk = wuk_ref[...]

    @pl.when(j < n_tiles)
    def _():
        c_bf = jnp.concatenate([r[...].astype(BF16) for r in c_refs], axis=0)
        pe = jnp.concatenate([r[...] for r in pe_refs], axis=0)
        _attend_tile(qx, wuk, c_bf, pe, cos_ref[...], sin_ref[...], None, m_ref, l_ref, acc_ref)

    @pl.when(j == n_tiles)
    def _():
        rows = qx.shape[0]
        npad = 128
        kv = cnew_ref.shape[1]
        c_bf = jnp.concatenate([cnew_ref[...], jnp.zeros((npad - dec_seq, kv), F32)], axis=0).astype(BF16)
        pe = jnp.concatenate([penew_ref[...], jnp.zeros((npad - dec_seq, QK_ROPE), F32)], axis=0)
        qpos = lax.broadcasted_iota(jnp.int32, (rows, npad), 0) % dec_seq
        kpos = lax.broadcasted_iota(jnp.int32, (rows, npad), 1)
        _attend_tile(qx, wuk, c_bf, pe, cosn_ref[...], sinn_ref[...], kpos <= qpos, m_ref, l_ref, acc_ref)
        o_ref[...] = acc_ref[...] / l_ref[...]


def sample_attention(qx, wukT, cache_c, cache_pe, page_table, cos_k, sin_k, cos_n, sin_n, c_new, pe_new,
                     *, dec_seq, pages_per_tile):
    db, n_pages = page_table.shape
    page = cache_c.shape[1]
    kv = cache_c.shape[2]
    n_pg = pages_per_tile
    assert n_pages % n_pg == 0
    n_tiles = n_pages // n_pg
    tk = n_pg * page
    rows = N_HEADS * dec_seq
    last = n_tiles - 1

    def page_map(i):
        return lambda b, j, pt: (pt[b, jnp.minimum(j, last) * n_pg + i], 0, 0)

    in_specs = [pl.BlockSpec((rows, qx.shape[1]), lambda b, j, pt: (b, 0)),
                pl.BlockSpec(wukT.shape, lambda b, j, pt: (0, 0))]
    in_specs += [pl.BlockSpec((None, page, kv), page_map(i)) for i in range(n_pg)]
    in_specs += [pl.BlockSpec((None, page, QK_ROPE), page_map(i)) for i in range(n_pg)]
    in_specs += [pl.BlockSpec((tk, QK_ROPE), lambda b, j, pt: (jnp.minimum(j, last), 0)),
                 pl.BlockSpec((tk, QK_ROPE), lambda b, j, pt: (jnp.minimum(j, last), 0)),
                 pl.BlockSpec(cos_n.shape, lambda b, j, pt: (0, 0)),
                 pl.BlockSpec(sin_n.shape, lambda b, j, pt: (0, 0)),
                 pl.BlockSpec((dec_seq, kv), lambda b, j, pt: (b, 0)),
                 pl.BlockSpec((dec_seq, QK_ROPE), lambda b, j, pt: (b, 0))]
    grid_spec = pltpu.PrefetchScalarGridSpec(
        num_scalar_prefetch=1, grid=(db, n_tiles + 1), in_specs=in_specs,
        out_specs=pl.BlockSpec((rows, kv), lambda b, j, pt: (b, 0)),
        scratch_shapes=[pltpu.VMEM((rows, 1), F32), pltpu.VMEM((rows, 1), F32), pltpu.VMEM((rows, kv), F32)])
    return pl.pallas_call(
        functools.partial(_sample_attn_body, n_pg=n_pg, n_tiles=n_tiles, dec_seq=dec_seq),
        grid_spec=grid_spec, out_shape=jax.ShapeDtypeStruct((db * rows, kv), F32),
        compiler_params=_cparams(2), name="sample_attention")(
            page_table, qx, wukT, *([cache_c] * n_pg), *([cache_pe] * n_pg),
            cos_k, sin_k, cos_n, sin_n, c_new, pe_new)


def _top_rows(s, cidx=None):
    n_rows = s.shape[0]
    iota = lax.broadcasted_iota(jnp.int32, s.shape, 0)
    vals, idxs = [], []
    for _ in range(PEER_TOPK):
        m = jnp.max(s, axis=0, keepdims=True)
        am = jnp.min(jnp.where(s == m, iota, n_rows), axis=0, keepdims=True)
        sel = iota == am
        vals.append(m)
        idxs.append(am if cidx is None else jnp.sum(jnp.where(sel, cidx, 0), axis=0, keepdims=True))
        s = jnp.where(sel, -jnp.inf, s)
    return vals, idxs


def _peer_topk_body(q_ref, sk_ref, idx_ref, g_ref):
    def head(h, carry):
        tops = []
        for p in range(2):
            col = pl.multiple_of((h * 2 + p) * PEER_HALF, PEER_HALF)
            s = lax.dot_general(sk_ref[h * 2 + p], q_ref[:, pl.ds(col, PEER_HALF)], _NT,
                                preferred_element_type=F32)
            tops.append(_top_rows(s))
        (v1, i1), (v2, i2) = tops
        v2a = jnp.concatenate(v2, axis=0)
        i2a = jnp.concatenate(i2, axis=0)
        cand = jnp.concatenate([v1[i] + v2a for i in range(PEER_TOPK)], axis=0)
        cidx = jnp.concatenate([i1[i] * N_KEYS + i2a for i in range(PEER_TOPK)], axis=0)
        best, eidx = _top_rows(cand, cidx)
        b = jnp.concatenate(best, axis=0)
        e = jnp.exp(b - best[0])
        row = pl.multiple_of(h * PEER_TOPK, PEER_TOPK)
        idx_ref[pl.ds(row, PEER_TOPK), :] = jnp.concatenate(eidx, axis=0)
        g_ref[pl.ds(row, PEER_TOPK), :] = e / jnp.sum(e, axis=0, keepdims=True)
        return carry

    lax.fori_loop(0, PEER_HEADS, head, 0)


def peer_topk(q, subkeys, *, tb):
    t = q.shape[0]
    rows = PEER_HEADS * PEER_TOPK
    return pl.pallas_call(
        _peer_topk_body, grid=(t // tb,),
        in_specs=[pl.BlockSpec((tb, q.shape[1]), lambda i: (i, 0)),
                  pl.BlockSpec(subkeys.shape, lambda i: (0, 0, 0))],
        out_specs=[pl.BlockSpec((rows, tb), lambda i: (0, i)), pl.BlockSpec((rows, tb), lambda i: (0, i))],
        out_shape=[jax.ShapeDtypeStruct((rows, t), jnp.int32), jax.ShapeDtypeStruct((rows, t), F32)],
        compiler_params=_cparams(1), name="peer_topk")(q, subkeys)


PEER_ROWS = PEER_HEADS * PEER_TOPK
PEER_TB = 16


def _unpack_pairs(w):
    lo = lax.bitcast_convert_type(lax.shift_left(w, jnp.uint32(16)), F32)
    hi = lax.bitcast_convert_type(lax.bitwise_and(w, jnp.uint32(0xFFFF0000)), F32)
    return lo, hi


def _peer_gather_body(idx_ref, idx_next_ref, g_ref, hn_ref, uv_hbm, o_ref, buf_a, buf_b, sem_a, sem_b, *, n_steps):
    i = pl.program_id(0)
    half = hn_ref.shape[1] // 2
    n_chunk = half // 128
    blk = PEER_TB * PEER_ROWS

    def row_copy(e, buf, sem, t, j):
        return pltpu.make_async_copy(uv_hbm.at[e], buf.at[t, pl.ds(j, 1), :], sem)

    def wait_all(buf, sem):
        def wait_tok(t, c):
            for j in range(PEER_ROWS):
                row_copy(0, buf, sem, t, j).wait()
            return c
        lax.fori_loop(0, PEER_TB, wait_tok, 0)

    @pl.when(i == 0)
    def _():
        def tok(t, c):
            for j in range(PEER_ROWS):
                row_copy(idx_ref[0, t * PEER_ROWS + j], buf_a, sem_a, t, j).start()
            return c
        lax.fori_loop(0, PEER_TB, tok, 0)

    lane = lax.broadcasted_iota(jnp.int32, (PEER_ROWS, PEER_TB), 1)

    def process(k, buf, sem, buf_next, sem_next, ids_next_ref, ids_next_base):
        wait_all(buf, sem)
        r0 = k * PEER_TB
        act = jnp.zeros((PEER_ROWS, PEER_TB), F32)
        for t in range(PEER_TB):
            for j in range(PEER_ROWS):
                e = ids_next_ref[0, ids_next_base + t * PEER_ROWS + j]
                row_copy(e, buf_next, sem_next, t, j).start()
            acc = jnp.zeros((PEER_ROWS, 128), F32)
            for c in range(n_chunk):
                lo, hi = _unpack_pairs(buf[t, :, c * 128:(c + 1) * 128])
                acc = acc + lo * hn_ref[r0 + t:r0 + t + 1, c * 128:(c + 1) * 128]
                acc = acc + hi * hn_ref[r0 + t:r0 + t + 1, half + c * 128:half + (c + 1) * 128]
            act = jnp.where(lane == t, jnp.sum(acc, axis=-1, keepdims=True), act)
        gelu = 0.5 * act * (1.0 + lax.erf(act * np.float32(np.sqrt(0.5))))
        coef = g_ref[k] * gelu
        for t in range(PEER_TB):
            ct = coef[:, t:t + 1]
            for c in range(n_chunk):
                lo, hi = _unpack_pairs(buf[t, :, half + c * 128:half + (c + 1) * 128])
                o_ref[r0 + t:r0 + t + 1, c * 128:(c + 1) * 128] = jnp.sum(lo * ct, axis=0, keepdims=True)
                o_ref[r0 + t:r0 + t + 1, half + c * 128:half + (c + 1) * 128] = jnp.sum(hi * ct, axis=0, keepdims=True)

    process(0, buf_a, sem_a, buf_b, sem_b, idx_ref, blk)
    process(1, buf_b, sem_b, buf_a, sem_a, idx_next_ref, 0)

    @pl.when(i == n_steps - 1)
    def _():
        wait_all(buf_a, sem_a)


def peer_gather(idx, g, hn_eo, uv):
    n_steps = idx.shape[0]
    t, d = hn_eo.shape
    last = n_steps - 1
    ids = 2 * PEER_TB * PEER_ROWS
    return pl.pallas_call(
        functools.partial(_peer_gather_body, n_steps=n_steps), grid=(n_steps,),
        in_specs=[pl.BlockSpec((None, 1, ids), lambda i: (i, 0, 0), memory_space=pltpu.SMEM),
                  pl.BlockSpec((None, 1, ids), lambda i: (jnp.minimum(i + 1, last), 0, 0), memory_space=pltpu.SMEM),
                  pl.BlockSpec((2, PEER_ROWS, PEER_TB), lambda i: (i, 0, 0)),
                  pl.BlockSpec((2 * PEER_TB, d), lambda i: (i, 0)),
                  pl.BlockSpec(memory_space=pl.ANY)],
        out_specs=pl.BlockSpec((2 * PEER_TB, d), lambda i: (i, 0)),
        out_shape=jax.ShapeDtypeStruct((t, d), F32),
        scratch_shapes=[pltpu.VMEM((PEER_TB, PEER_ROWS, d), jnp.uint32), pltpu.VMEM((PEER_TB, PEER_ROWS, d), jnp.uint32),
                        pltpu.SemaphoreType.DMA, pltpu.SemaphoreType.DMA],
        compiler_params=_cparams(1), name="peer_gather")(idx, idx, g, hn_eo, uv)


def _pack_pairs(a):
    b = lax.bitcast_convert_type(a.astype(BF16), jnp.uint16).astype(jnp.uint32)
    return b[:, 0::2] | (b[:, 1::2] << 16)


def _rope_tables(pos):
    half = QK_ROPE // 2
    freq = ROPE_THETA ** (-jnp.arange(half, dtype=F32) / half)
    ang = pos.astype(F32)[:, None] * freq[None, :]
    cos, sin = jnp.cos(ang), jnp.sin(ang)
    return jnp.concatenate([cos, cos], axis=-1), jnp.concatenate([sin, sin], axis=-1)


def _pad_lanes(t):
    return jnp.concatenate([t, jnp.zeros_like(t)], axis=-1)


def _rot_cols(w_rope, g_rope):
    half = QK_ROPE // 2
    return jnp.concatenate([-w_rope[..., half:] * g_rope[half:], w_rope[..., :half] * g_rope[:half]], axis=-1)


def _head_gain(g):
    return jnp.tile(jnp.concatenate([g, jnp.zeros((HEAD_PAD - QK_DIM,), F32)]), N_HEADS)


def _layer(x, p, hist, cache, wts, *, batch, seq, start_pos):
    t = x.shape[0]
    d_model = x.shape[1]
    q_lora = wts["q_norm_g"].shape[0]
    kv_lora = wts["kv_norm_g"].shape[0]
    pool_w = wts["pool_scale"].shape[0]
    bm = min(1024, t)
    ident = lambda acc: acc

    z = fused_mm(x, wts["w_in"], g1=wts["norm1_g"], epilogue=ident, out_dtype=F32, bm=bm, bn=1280, name="in_proj")
    off_u = q_lora + kv_lora
    off_ga = off_u + pool_w
    off_gb = off_ga + d_model
    off_pe = off_gb + d_model
    pe_raw = z[:, off_pe:off_pe + QK_ROPE]
    c_lat, ckv = latent_rows(z, wts["kv_norm_g"], c_blk=q_lora // kv_lora, pe_blk=off_pe // 128,
                             kv_lora=kv_lora, bm=min(512, t))

    pos = start_pos + jnp.arange(seq, dtype=jnp.int32)
    cos2, sin2 = _rope_tables(pos)
    n_rep = 1 if cache is None else batch
    ca, sa = jnp.tile(_pad_lanes(cos2), (n_rep, 1)), jnp.tile(_pad_lanes(sin2), (n_rep, 1))
    hb = min(512, t)
    q = head_mm(z, wts["w_q"], wts["q_gain"], ca, sa, scale=QK_DIM ** -0.5, bm=hb, bn=1024,
                g=wts["q_norm_g"], a_cols=(q_lora, 0), name="q_proj")

    if cache is None:
        k = head_mm(ckv, wts["w_k"], wts["k_gain"], ca, sa, scale=1.0, bm=hb, bn=1024, name="k_proj")
        v = fused_mm(ckv, wts["w_v"], a1_cols=(kv_lora, 0), epilogue=ident, out_dtype=BF16, bm=bm, bn=1024,
                     name="v_proj")
        o = flash_attention(q, k, v, batch=batch, seq=seq, tq=min(1024, seq), tk=min(512, seq))
    else:
        cache_c, cache_pe, page_table = cache
        past = page_table.shape[1] * cache_c.shape[1]
        qx = block_mm(q, wts["w_absorb"], out_dtype=BF16, bm=bm, name="q_absorb")
        xw = kv_lora + 2 * QK_ROPE
        qx = qx.reshape(batch, seq, N_HEADS, xw).transpose(0, 2, 1, 3).reshape(t * N_HEADS, xw)
        cos_k, sin_k = _rope_tables(jnp.arange(past, dtype=jnp.int32))
        pad = jnp.zeros((128 - seq, QK_ROPE), F32)
        cos_n, sin_n = jnp.concatenate([cos2, pad]), jnp.concatenate([sin2, pad])
        o_lat = sample_attention(qx, wts["w_ukT"], cache_c, cache_pe, page_table, cos_k, sin_k, cos_n, sin_n,
                                 c_lat, pe_raw, dec_seq=seq, pages_per_tile=min(8, page_table.shape[1]))
        o_lat = o_lat.reshape(batch, N_HEADS, seq, kv_lora).transpose(1, 0, 2, 3).reshape(N_HEADS, t, kv_lora)
        o = block_mm(o_lat, wts["w_uv_heads"], out_dtype=BF16, bm=bm, name="o_up")

    d = pool_diff(z, hist, u_blk=off_u // pool_w, width=pool_w, batch=batch, seq=seq,
                  tb=min(512, seq), start_pos=start_pos)
    y = block_mm(d, wts["w_pool"], out_dtype=BF16, bm=bm, col_scale=wts["pool_scale"], name="pool_map")

    def merge_epi(a, b, ga, gb):
        return jax.nn.sigmoid(ga) * a + jax.nn.sigmoid(gb) * b

    mm = fused_mm(o, wts["w_a"], a2=y, w2=wts["w_b"], epilogue=merge_epi, out_dtype=BF16, bm=bm, bn=1024,
                  extras=(("tile", z, off_ga // 1024), ("tile", z, off_gb // 1024)), name="merge")
    x1 = fused_mm(mm, wts["w_out"], epilogue=lambda acc, xr: xr + acc, out_dtype=F32, bm=bm, bn=1024,
                  extras=(("tile", x, 0),), name="out_proj")

    q_peer, hn = fused_mm(x1, wts["w_peer_q"], g1=wts["norm2_g"], epilogue=ident, out_dtype=BF16,
                          bm=min(512, t), bn=1024, side_dtype=F32, name="peer_q")
    idx_t, g_t = peer_topk(q_peer, wts["peer_subkeys"], tb=min(256, t))
    nb = t // PEER_TB
    idx = idx_t.T.reshape(nb // 2, 1, 2 * PEER_TB * PEER_ROWS)
    g = g_t.reshape(PEER_ROWS, nb, PEER_TB).transpose(1, 0, 2)
    hn_eo = jnp.concatenate([hn[:, 0::2], hn[:, 1::2]], axis=1)
    y_eo = peer_gather(idx, g, hn_eo, wts["peer_uv"])
    x2 = x1 + jnp.stack([y_eo[:, :d_model // 2], y_eo[:, d_model // 2:]], axis=-1).reshape(t, d_model)

    def ple_epi(gate, emb, xr):
        return xr + emb * jax.nn.sigmoid(gate)

    x3 = fused_mm(x2, wts["w_ple_gate"], g1=wts["norm3_g"], a2=p, w2=wts["w_ple"], epilogue=ple_epi,
                  out_dtype=F32, bm=bm, bn=1024, extras=(("tile", x2, 0),), name="ple")
    u_pool = z[:, off_u:off_u + pool_w].reshape(batch, seq, pool_w)
    return x3, c_lat, pe_raw, u_pool


def _prep_weights(i, norm1_g, w_in, q_norm_g, w_q_up, kv_norm_g, w_kv_up, q_head_g, k_head_g, w_pool_map,
                  pool_scale, w_a_proj, w_b_proj, w_out, norm2_g, w_peer_q, peer_subkeys, peer_u, peer_v,
                  norm3_g, w_ple, w_ple_gate):
    d_model = w_in.shape[1]
    q_lora, kv_lora = q_norm_g.shape[1], kv_norm_g.shape[1]
    pool_w = pool_scale.shape[1]
    s0, s1, s2, s3 = q_lora, q_lora + kv_lora, q_lora + kv_lora + QK_ROPE, q_lora + kv_lora + QK_ROPE + pool_w
    wi = w_in[i]
    n_used = wi.shape[1]
    n_pad = -(-n_used // 1280) * 1280
    w_in_p = jnp.concatenate([wi[:, :s1], wi[:, s2:], wi[:, s1:s2], jnp.zeros((d_model, n_pad - n_used), F32)], axis=1)
    qg, kg = q_head_g[i], k_head_g[i]
    wq = w_q_up[i].reshape(q_lora, N_HEADS, QK_DIM)
    wq_r = wq[..., QK_NOPE:]
    w_q = jnp.concatenate([wq[..., :QK_NOPE], wq_r, _rot_cols(wq_r, qg[QK_NOPE:])], axis=-1).reshape(q_lora, -1)
    wkv = w_kv_up[i].reshape(kv_lora, N_HEADS, QK_NOPE + V_DIM)
    w_uk, w_uv = wkv[..., :QK_NOPE], wkv[..., QK_NOPE:]
    eye = jnp.eye(QK_ROPE, dtype=F32)
    pe_cols = jnp.concatenate([jnp.zeros((QK_ROPE, QK_NOPE), F32), eye, _rot_cols(eye, kg[QK_NOPE:])], axis=-1)
    top = jnp.concatenate([w_uk, jnp.zeros((kv_lora, N_HEADS, HEAD_PAD - QK_NOPE), F32)], axis=-1)
    mid = jnp.broadcast_to(pe_cols[:, None, :], (QK_ROPE, N_HEADS, HEAD_PAD))
    bot = jnp.zeros((128 - QK_ROPE, N_HEADS, HEAD_PAD), F32)
    w_k = jnp.concatenate([top, mid, bot], axis=0).reshape(kv_lora + 128, -1)
    g_kn, g_kr = kg[:QK_NOPE], kg[QK_NOPE:]
    half = QK_ROPE // 2
    absorb = jnp.transpose(w_uk, (1, 2, 0)) * g_kn[None, :, None]
    swap = jnp.concatenate([jnp.concatenate([jnp.zeros((half, half), F32), -jnp.eye(half, dtype=F32)], axis=1),
                            jnp.concatenate([jnp.eye(half, dtype=F32), jnp.zeros((half, half), F32)], axis=1)], axis=0)
    rope_map = jnp.concatenate([eye * g_kr[None, :], swap * g_kr[None, :]], axis=1)
    w_absorb = jnp.zeros((N_HEADS, HEAD_PAD, kv_lora + 2 * QK_ROPE), F32)
    w_absorb = w_absorb.at[:, :QK_NOPE, :kv_lora].set(absorb)
    w_absorb = w_absorb.at[:, QK_NOPE:QK_DIM, kv_lora:].set(jnp.broadcast_to(rope_map, (N_HEADS,) + rope_map.shape))
    return {
        "norm1_g": norm1_g[i], "w_in": w_in_p.astype(BF16), "q_norm_g": q_norm_g[i], "kv_norm_g": kv_norm_g[i],
        "w_q": w_q.astype(BF16), "q_gain": _head_gain(qg), "w_k": w_k.astype(BF16), "k_gain": _head_gain(kg),
        "w_v": w_uv.reshape(kv_lora, -1).astype(BF16),
        "w_ukT": jnp.transpose(w_uk, (1, 2, 0)).reshape(N_HEADS * QK_NOPE, kv_lora).astype(BF16),
        "w_uv_heads": jnp.transpose(w_uv, (1, 0, 2)).astype(BF16), "w_absorb": w_absorb.astype(BF16),
        "w_pool": w_pool_map[i].astype(BF16), "pool_scale": pool_scale[i],
        "w_a": w_a_proj[i].astype(BF16), "w_b": w_b_proj[i].astype(BF16), "w_out": w_out[i].astype(BF16),
        "norm2_g": norm2_g[i], "w_peer_q": w_peer_q[i].astype(BF16),
        "peer_subkeys": peer_subkeys[i].reshape(PEER_HEADS * 2, N_KEYS, PEER_HALF).astype(BF16),
        "peer_uv": jnp.concatenate([_pack_pairs(peer_u[i]), _pack_pairs(peer_v[i])], axis=1)[:, None, :],
        "norm3_g": norm3_g[i],
        "w_ple": w_ple[i].astype(BF16), "w_ple_gate": w_ple_gate[i].astype(BF16),
    }


def kernel(x_prompt, x_sample, p_prompt, p_sample, cache_kv_latent, cache_k_rope, state_pool, page_table, norm1_g, w_in, q_norm_g, w_q_up, kv_norm_g, w_kv_up, q_head_g, k_head_g, w_pool_map, pool_scale, w_a_proj, w_b_proj, w_out, norm2_g, w_peer_q, peer_subkeys, peer_u, peer_v, norm3_g, w_ple, w_ple_gate):
    b, s, d_model = x_prompt.shape
    db, ds, _ = x_sample.shape
    depth = w_in.shape[0]
    pool_w = pool_scale.shape[1]
    pool_pad = state_pool.shape[2]
    past_len = page_table.shape[1] * cache_kv_latent.shape[2]
    xp = x_prompt.reshape(b * s, d_model)
    xs = x_sample.reshape(db * ds, d_model)
    outs = [[] for _ in range(6)]
    for i in range(depth):
        wts = _prep_weights(i, norm1_g, w_in, q_norm_g, w_q_up, kv_norm_g, w_kv_up, q_head_g, k_head_g, w_pool_map,
                            pool_scale, w_a_proj, w_b_proj, w_out, norm2_g, w_peer_q, peer_subkeys, peer_u, peer_v,
                            norm3_g, w_ple, w_ple_gate)
        hist_p = jnp.zeros((b, POOL_HIST, pool_w), F32)
        xp, lat, pe, u = _layer(xp, p_prompt[i].reshape(b * s, -1), hist_p, None, wts, batch=b, seq=s, start_pos=0)
        outs[0].append(lat.reshape(b, s, -1))
        outs[1].append(pe.reshape(b, s, -1))
        outs[2].append(jnp.concatenate([hist_p[:, POOL_HIST - pool_pad:], u], axis=1)[:, -pool_pad:])
        hist_s = jnp.concatenate([jnp.zeros((db, POOL_HIST - pool_pad, pool_w), F32), state_pool[i]], axis=1)
        cache = (cache_kv_latent[i], cache_k_rope[i], page_table)
        xs, lat, pe, u = _layer(xs, p_sample[i].reshape(db * ds, -1), hist_s, cache, wts, batch=db, seq=ds,
                                start_pos=past_len)
        outs[3].append(lat.reshape(db, ds, -1))
        outs[4].append(pe.reshape(db, ds, -1))
        outs[5].append(jnp.concatenate([state_pool[i], u], axis=1)[:, -pool_pad:])
    return (xp.reshape(b, s, d_model), xs.reshape(db, ds, d_model)) + tuple(jnp.stack(o) for o in outs)
```

```python
import functools

import jax
import jax.numpy as jnp
import numpy as np
from jax import lax
from jax.experimental import pallas as pl
from jax.experimental.pallas import tpu as pltpu

F32 = jnp.float32
BF16 = jnp.bfloat16

N_HEADS = 16
QK_NOPE = 128
QK_ROPE = 64
QK_DIM = QK_NOPE + QK_ROPE
V_DIM = 128
HEAD_PAD = 256
ROPE_THETA = 10000.0
POOL_WINDOWS = (2, 4, 8, 16)
POOL_GROUP = 256
POOL_HIST = 16
PEER_HEADS = 8
N_KEYS = 128
PEER_TOPK = 16
PEER_HALF = 128
PEER_BLOCK = 128
EPS = 1e-6
NEG_BIG = -1e30

VMEM_LIMIT_BYTES = 56 * 1024 * 1024
_NT = (((1,), (1,)), ((), ()))


def _cparams(n_axes):
    return pltpu.CompilerParams(dimension_semantics=("arbitrary",) * n_axes,
                                vmem_limit_bytes=VMEM_LIMIT_BYTES)


def _fused_mm_body(*refs, has_norm, has_a2, n_extras, epilogue, has_side):
    it = iter(refs)
    a1_ref = next(it)
    g1_ref = next(it) if has_norm else None
    w1_ref = next(it)
    a2_ref = next(it) if has_a2 else None
    w2_ref = next(it) if has_a2 else None
    extra_refs = [next(it) for _ in range(n_extras)]
    out_ref = next(it)
    side_ref = next(it) if has_side else None
    h_ref = next(it) if has_norm else None

    if has_norm:
        @pl.when(pl.program_id(1) == 0)
        def _():
            x = a1_ref[...].astype(F32)
            ms = jnp.mean(x * x, axis=-1, keepdims=True)
            h = x * lax.rsqrt(ms + EPS) * g1_ref[...]
            h_ref[...] = h.astype(BF16)
            if has_side:
                side_ref[...] = h.astype(side_ref.dtype)

        lhs = h_ref[...]
    else:
        lhs = a1_ref[...].astype(BF16)
    accs = [jnp.dot(lhs, w1_ref[...], preferred_element_type=F32)]
    if has_a2:
        accs.append(jnp.dot(a2_ref[...].astype(BF16), w2_ref[...], preferred_element_type=F32))
    out_ref[...] = epilogue(*accs, *[r[...] for r in extra_refs]).astype(out_ref.dtype)


def fused_mm(a1, w1, *, epilogue, out_dtype, bm, bn, g1=None, a1_cols=None, a2=None, w2=None,
             a2_cols=None, extras=(), side_dtype=None, name="fused_mm"):
    m = a1.shape[0]
    k1, n = w1.shape
    assert m % bm == 0 and n % bn == 0
    a1_idx = 0 if a1_cols is None else a1_cols[1]
    assert (a1.shape[1] if a1_cols is None else a1_cols[0]) == k1
    has_norm, has_a2, has_side = g1 is not None, a2 is not None, side_dtype is not None
    in_specs = [pl.BlockSpec((bm, k1), lambda i, j: (i, a1_idx))]
    args = [a1]
    if has_norm:
        in_specs.append(pl.BlockSpec((1, k1), lambda i, j: (0, 0)))
        args.append(g1.reshape(1, k1).astype(F32))
    in_specs.append(pl.BlockSpec((k1, bn), lambda i, j: (0, j)))
    args.append(w1)
    if has_a2:
        k2 = w2.shape[0]
        a2_idx = 0 if a2_cols is None else a2_cols[1]
        in_specs += [pl.BlockSpec((bm, k2), lambda i, j: (i, a2_idx)),
                     pl.BlockSpec((k2, bn), lambda i, j: (0, j))]
        args += [a2, w2]
    for ex in extras:
        if ex[0] == "tile":
            off = ex[2]
            in_specs.append(pl.BlockSpec((bm, bn), lambda i, j, off=off: (i, j + off)))
            args.append(ex[1])
        else:
            in_specs.append(pl.BlockSpec((1, bn), lambda i, j: (0, j)))
            args.append(ex[1].reshape(1, n).astype(F32))
    out_shape = [jax.ShapeDtypeStruct((m, n), out_dtype)]
    out_specs = [pl.BlockSpec((bm, bn), lambda i, j: (i, j))]
    if has_side:
        out_shape.append(jax.ShapeDtypeStruct((m, k1), side_dtype))
        out_specs.append(pl.BlockSpec((bm, k1), lambda i, j: (i, 0)))
    scratch = [pltpu.VMEM((bm, k1), BF16)] if has_norm else []
    res = pl.pallas_call(
        functools.partial(_fused_mm_body, has_norm=has_norm, has_a2=has_a2, n_extras=len(extras),
                          epilogue=epilogue, has_side=has_side),
        grid=(m // bm, n // bn), in_specs=in_specs, out_specs=out_specs, out_shape=out_shape,
        scratch_shapes=scratch, compiler_params=_cparams(2), name=name)(*args)
    return res if has_side else res[0]


def _latent_rows_body(c_ref, pe_ref, g_ref, lat_ref, ext_ref):
    x = c_ref[...]
    ms = jnp.mean(x * x, axis=-1, keepdims=True)
    lat = x * lax.rsqrt(ms + EPS) * g_ref[...]
    lat_ref[...] = lat
    kv = lat_ref.shape[1]
    ext_ref[:, :kv] = lat.astype(BF16)
    ext_ref[:, kv:] = pe_ref[...].astype(BF16)


def latent_rows(z, g, *, c_blk, pe_blk, kv_lora, bm):
    m = z.shape[0]
    return pl.pallas_call(
        _latent_rows_body, grid=(m // bm,),
        in_specs=[pl.BlockSpec((bm, kv_lora), lambda i: (i, c_blk)),
                  pl.BlockSpec((bm, 128), lambda i: (i, pe_blk)),
                  pl.BlockSpec((1, kv_lora), lambda i: (0, 0))],
        out_specs=[pl.BlockSpec((bm, kv_lora), lambda i: (i, 0)),
                   pl.BlockSpec((bm, kv_lora + 128), lambda i: (i, 0))],
        out_shape=[jax.ShapeDtypeStruct((m, kv_lora), F32),
                   jax.ShapeDtypeStruct((m, kv_lora + 128), BF16)],
        compiler_params=_cparams(1), name="latent_rows")(z, z, g.reshape(1, kv_lora).astype(F32))


def _head_mm_body(*refs, has_norm, scale):
    if has_norm:
        a_ref, g_ref, w_ref, gv_ref, ca_ref, sa_ref, out_ref, h_ref = refs

        @pl.when(pl.program_id(1) == 0)
        def _():
            x = a_ref[...].astype(F32)
            ms = jnp.mean(x * x, axis=-1, keepdims=True)
            h_ref[...] = (x * lax.rsqrt(ms + EPS) * g_ref[...]).astype(BF16)

        lhs = h_ref[...]
    else:
        a_ref, w_ref, gv_ref, ca_ref, sa_ref, out_ref = refs
        lhs = a_ref[...].astype(BF16)
    acc = jnp.dot(lhs, w_ref[...], preferred_element_type=F32)
    bm, bn = acc.shape
    ca = ca_ref[...]
    sa = sa_ref[...]
    rope_lane = lax.broadcasted_iota(jnp.int32, (bm, 128), 1) < QK_ROPE
    for h in range(bn // HEAD_PAD):
        c0 = h * HEAD_PAD
        z1 = acc[:, c0:c0 + 128]
        z2 = acc[:, c0 + 128:c0 + 256]
        z2r = jnp.where(rope_lane, z2, 0.0)
        ss = jnp.sum(z1 * z1, axis=-1, keepdims=True) + jnp.sum(z2r * z2r, axis=-1, keepdims=True)
        r = lax.rsqrt(ss * (1.0 / QK_DIM) + EPS) * scale
        out_ref[:, c0:c0 + 128] = (z1 * gv_ref[:, c0:c0 + 128] * r).astype(out_ref.dtype)
        rot = pltpu.roll(z2, 64, axis=1)
        o2 = (z2 * gv_ref[:, c0 + 128:c0 + 256] * ca + rot * sa) * r
        out_ref[:, c0 + 128:c0 + 256] = o2.astype(out_ref.dtype)


def head_mm(a, w, gv, ca, sa, *, scale, bm, bn, g=None, a_cols=None, name="head_mm"):
    m = a.shape[0]
    k, n = w.shape
    a_idx = 0 if a_cols is None else a_cols[1]
    n_tab = ca.shape[0] // bm
    has_norm = g is not None
    in_specs = [pl.BlockSpec((bm, k), lambda i, j: (i, a_idx))]
    args = [a]
    if has_norm:
        in_specs.append(pl.BlockSpec((1, k), lambda i, j: (0, 0)))
        args.append(g.reshape(1, k).astype(F32))
    in_specs += [pl.BlockSpec((k, bn), lambda i, j: (0, j)),
                 pl.BlockSpec((1, bn), lambda i, j: (0, j)),
                 pl.BlockSpec((bm, 128), lambda i, j: (i % n_tab, 0)),
                 pl.BlockSpec((bm, 128), lambda i, j: (i % n_tab, 0))]
    args += [w, gv.reshape(1, n).astype(F32), ca, sa]
    return pl.pallas_call(
        functools.partial(_head_mm_body, has_norm=has_norm, scale=scale),
        grid=(m // bm, n // bn), in_specs=in_specs,
        out_specs=pl.BlockSpec((bm, bn), lambda i, j: (i, j)),
        out_shape=jax.ShapeDtypeStruct((m, n), BF16),
        scratch_shapes=[pltpu.VMEM((bm, k), BF16)] if has_norm else [],
        compiler_params=_cparams(2), name=name)(*args)


def _flash_body(qt_ref, kt_ref, q_ref, k_ref, v_ref, o_ref, m_ref, l_ref, acc_ref, *, tq, tk):
    p = pl.program_id(2)
    qi = qt_ref[p]
    ki = kt_ref[p]

    @pl.when(ki == 0)
    def _():
        m_ref[...] = jnp.full(m_ref.shape, NEG_BIG, F32)
        l_ref[...] = jnp.zeros(l_ref.shape, F32)
        acc_ref[...] = jnp.zeros(acc_ref.shape, F32)

    s = lax.dot_general(q_ref[...], k_ref[...], _NT, preferred_element_type=F32)
    row = qi * tq + lax.broadcasted_iota(jnp.int32, (tq, tk), 0)
    col = ki * tk + lax.broadcasted_iota(jnp.int32, (tq, tk), 1)
    s = jnp.where(col <= row, s, NEG_BIG)
    m_old = m_ref[...]
    m_new = jnp.maximum(m_old, jnp.max(s, axis=-1, keepdims=True))
    alpha = jnp.exp(m_old - m_new)
    pr = jnp.exp(s - m_new)
    l_ref[...] = alpha * l_ref[...] + jnp.sum(pr, axis=-1, keepdims=True)
    acc_ref[...] = alpha * acc_ref[...] + jnp.dot(pr.astype(BF16), v_ref[...], preferred_element_type=F32)
    m_ref[...] = m_new

    @pl.when(ki == ((qi + 1) * tq - 1) // tk)
    def _():
        o_ref[...] = (acc_ref[...] / l_ref[...]).astype(o_ref.dtype)


def flash_attention(q, k, v, *, batch, seq, tq, tk):
    nq, nk = seq // tq, seq // tk
    pairs = [(qi, ki) for qi in range(nq) for ki in range(((qi + 1) * tq - 1) // tk + 1)]
    qt = jnp.asarray([p[0] for p in pairs], jnp.int32)
    kt = jnp.asarray([p[1] for p in pairs], jnp.int32)
    grid_spec = pltpu.PrefetchScalarGridSpec(
        num_scalar_prefetch=2, grid=(batch, N_HEADS, len(pairs)),
        in_specs=[pl.BlockSpec((tq, HEAD_PAD), lambda b, h, p, qt, kt: (b * nq + qt[p], h)),
                  pl.BlockSpec((tk, HEAD_PAD), lambda b, h, p, qt, kt: (b * nk + kt[p], h)),
                  pl.BlockSpec((tk, V_DIM), lambda b, h, p, qt, kt: (b * nk + kt[p], h))],
        out_specs=pl.BlockSpec((tq, V_DIM), lambda b, h, p, qt, kt: (b * nq + qt[p], h)),
        scratch_shapes=[pltpu.VMEM((tq, 1), F32), pltpu.VMEM((tq, 1), F32), pltpu.VMEM((tq, V_DIM), F32)])
    return pl.pallas_call(
        functools.partial(_flash_body, tq=tq, tk=tk), grid_spec=grid_spec,
        out_shape=jax.ShapeDtypeStruct((batch * seq, N_HEADS * V_DIM), BF16),
        compiler_params=_cparams(3), name="flash_attention")(qt, kt, q, k, v)


def _block_mm_body(x_ref, w_ref, s_ref, o_ref):
    acc = jnp.dot(x_ref[...].astype(BF16), w_ref[...], preferred_element_type=F32)
    o_ref[...] = (acc * s_ref[...]).astype(o_ref.dtype)


def block_mm(x, w, *, out_dtype, bm, col_scale=None, name="block_mm"):
    nh, k, n = w.shape
    if x.ndim == 2:
        m = x.shape[0]
        x_spec = pl.BlockSpec((bm, k), lambda i, h: (i, h))
    else:
        m = x.shape[1]
        x_spec = pl.BlockSpec((None, bm, k), lambda i, h: (h, i, 0))
    if col_scale is None:
        col_scale = jnp.ones((nh * n,), F32)
    return pl.pallas_call(
        _block_mm_body, grid=(m // bm, nh),
        in_specs=[x_spec, pl.BlockSpec((None, k, n), lambda i, h: (h, 0, 0)),
                  pl.BlockSpec((1, n), lambda i, h: (0, h))],
        out_specs=pl.BlockSpec((bm, n), lambda i, h: (i, h)),
        out_shape=jax.ShapeDtypeStruct((m, nh * n), out_dtype),
        compiler_params=_cparams(2), name=name)(x, w, col_scale.reshape(1, nh * n).astype(F32))


def _pool_diff_body(hist_ref, u_ref, d_ref, ext_ref, *, tb, start_pos):
    t = pl.program_id(1)

    @pl.when(t == 0)
    def _():
        ext_ref[0:POOL_HIST, :] = hist_ref[...]

    u = u_ref[...]
    ext_ref[POOL_HIST:, :] = u
    pos = start_pos + t * tb + lax.broadcasted_iota(jnp.int32, (tb, POOL_GROUP), 0) + 1
    for gi, w in enumerate(POOL_WINDOWS):
        c0 = gi * POOL_GROUP
        s = u[:, c0:c0 + POOL_GROUP]
        for j in range(1, w):
            s = s + ext_ref[POOL_HIST - j:POOL_HIST - j + tb, c0:c0 + POOL_GROUP]
        cnt = jnp.minimum(w, pos).astype(F32)
        d_ref[:, c0:c0 + POOL_GROUP] = (s / cnt - u[:, c0:c0 + POOL_GROUP]).astype(d_ref.dtype)
    ext_ref[0:POOL_HIST, :] = ext_ref[tb:tb + POOL_HIST, :]


def pool_diff(z, hist, *, u_blk, width, batch, seq, tb, start_pos):
    nt = seq // tb
    return pl.pallas_call(
        functools.partial(_pool_diff_body, tb=tb, start_pos=start_pos), grid=(batch, nt),
        in_specs=[pl.BlockSpec((None, POOL_HIST, width), lambda b, t: (b, 0, 0)),
                  pl.BlockSpec((tb, width), lambda b, t: (b * nt + t, u_blk))],
        out_specs=pl.BlockSpec((tb, width), lambda b, t: (b * nt + t, 0)),
        out_shape=jax.ShapeDtypeStruct((batch * seq, width), BF16),
        scratch_shapes=[pltpu.VMEM((tb + POOL_HIST, width), F32)],
        compiler_params=_cparams(2), name="pool_diff")(hist, z)


def _attend_tile(qx, wukT, c_bf, pe, cos2, sin2, valid, m_ref, l_ref, acc_ref):
    n = c_bf.shape[0]
    kv = c_bf.shape[1]
    kn_t = lax.dot_general(wukT, c_bf, _NT, preferred_element_type=F32)
    ssn = jnp.sum((kn_t * kn_t).reshape(N_HEADS, QK_NOPE, n), axis=1)
    ones = jnp.ones((8, QK_ROPE), BF16)
    sspe = lax.dot_general(ones, (pe * pe).astype(BF16), _NT, preferred_element_type=F32)[0:1]
    r = lax.rsqrt((ssn + sspe) * (1.0 / QK_DIM) + EPS)
    s = lax.dot_general(qx[:, :kv], c_bf, _NT, preferred_element_type=F32)
    s = s + lax.dot_general(qx[:, kv:kv + QK_ROPE], (pe * cos2).astype(BF16), _NT, preferred_element_type=F32)
    s = s + lax.dot_general(qx[:, kv + QK_ROPE:], (pe * sin2).astype(BF16), _NT, preferred_element_type=F32)
    rows = s.shape[0]
    nq = rows // N_HEADS
    s = (s.reshape(N_HEADS, nq, n) * r[:, None, :]).reshape(rows, n)
    if valid is not None:
        s = jnp.where(valid, s, NEG_BIG)
    m_old = m_ref[...]
    m_new = jnp.maximum(m_old, jnp.max(s, axis=-1, keepdims=True))
    alpha = jnp.exp(m_old - m_new)
    pr = jnp.exp(s - m_new)
    l_ref[...] = alpha * l_ref[...] + jnp.sum(pr, axis=-1, keepdims=True)
    acc_ref[...] = alpha * acc_ref[...] + jnp.dot(pr.astype(BF16), c_bf, preferred_element_type=F32)
    m_ref[...] = m_new


def _sample_attn_body(pt_ref, *refs, n_pg, n_tiles, dec_seq):
    qx_ref, wuk_ref = refs[0], refs[1]
    c_refs = refs[2:2 + n_pg]
    pe_refs = refs[2 + n_pg:2 + 2 * n_pg]
    cos_ref, sin_ref, cosn_ref, sinn_ref, cnew_ref, penew_ref, o_ref, m_ref, l_ref, acc_ref = refs[2 + 2 * n_pg:]
    j = pl.program_id(1)

    @pl.when(j == 0)
    def _():
        m_ref[...] = jnp.full(m_ref.shape, NEG_BIG, F32)
        l_ref[...] = jnp.zeros(l_ref.shape, F32)
        acc_ref[...] = jnp.zeros(acc_ref.shape, F32)

    qx = qx_ref[...]
    wuk = wuk_ref[...]

    @pl.when(j < n_tiles)
    def _():
        c_bf = jnp.concatenate([r[...].astype(BF16) for r in c_refs], axis=0)
        pe = jnp.concatenate([r[...] for r in pe_refs], axis=0)
        _attend_tile(qx, wuk, c_bf, pe, cos_ref[...], sin_ref[...], None, m_ref, l_ref, acc_ref)

    @pl.when(j == n_tiles)
    def _():
        rows = qx.shape[0]
        npad = 128
        kv = cnew_ref.shape[1]
        c_bf = jnp.concatenate([cnew_ref[...], jnp.zeros((npad - dec_seq, kv), F32)], axis=0).astype(BF16)
        pe = jnp.concatenate([penew_ref[...], jnp.zeros((npad - dec_seq, QK_ROPE), F32)], axis=0)
        qpos = lax.broadcasted_iota(jnp.int32, (rows, npad), 0) % dec_seq
        kpos = lax.broadcasted_iota(jnp.int32, (rows, npad), 1)
        _attend_tile(qx, wuk, c_bf, pe, cosn_ref[...], sinn_ref[...], kpos <= qpos, m_ref, l_ref, acc_ref)
        o_ref[...] = acc_ref[...] / l_ref[...]


def sample_attention(qx, wukT, cache_c, cache_pe, page_table, cos_k, sin_k, cos_n, sin_n, c_new, pe_new,
                     *, dec_seq, pages_per_tile):
    db, n_pages = page_table.shape
    page = cache_c.shape[1]
    kv = cache_c.shape[2]
    n_pg = pages_per_tile
    assert n_pages % n_pg == 0
    n_tiles = n_pages // n_pg
    tk = n_pg * page
    rows = N_HEADS * dec_seq
    last = n_tiles - 1

    def page_map(i):
        return lambda b, j, pt: (pt[b, jnp.minimum(j, last) * n_pg + i], 0, 0)

    in_specs = [pl.BlockSpec((rows, qx.shape[1]), lambda b, j, pt: (b, 0)),
                pl.BlockSpec(wukT.shape, lambda b, j, pt: (0, 0))]
    in_specs += [pl.BlockSpec((None, page, kv), page_map(i)) for i in range(n_pg)]
    in_specs += [pl.BlockSpec((None, page, QK_ROPE), page_map(i)) for i in range(n_pg)]
    in_specs += [pl.BlockSpec((tk, QK_ROPE), lambda b, j, pt: (jnp.minimum(j, last), 0)),
                 pl.BlockSpec((tk, QK_ROPE), lambda b, j, pt: (jnp.minimum(j, last), 0)),
                 pl.BlockSpec(cos_n.shape, lambda b, j, pt: (0, 0)),
                 pl.BlockSpec(sin_n.shape, lambda b, j, pt: (0, 0)),
                 pl.BlockSpec((dec_seq, kv), lambda b, j, pt: (b, 0)),
                 pl.BlockSpec((dec_seq, QK_ROPE), lambda b, j, pt: (b, 0))]
    grid_spec = pltpu.PrefetchScalarGridSpec(
        num_scalar_prefetch=1, grid=(db, n_tiles + 1), in_specs=in_specs,
        out_specs=pl.BlockSpec((rows, kv), lambda b, j, pt: (b, 0)),
        scratch_shapes=[pltpu.VMEM((rows, 1), F32), pltpu.VMEM((rows, 1), F32), pltpu.VMEM((rows, kv), F32)])
    return pl.pallas_call(
        functools.partial(_sample_attn_body, n_pg=n_pg, n_tiles=n_tiles, dec_seq=dec_seq),
        grid_spec=grid_spec, out_shape=jax.ShapeDtypeStruct((db * rows, kv), F32),
        compiler_params=_cparams(2), name="sample_attention")(
            page_table, qx, wukT, *([cache_c] * n_pg), *([cache_pe] * n_pg),
            cos_k, sin_k, cos_n, sin_n, c_new, pe_new)


def _top_rows(s, cidx=None):
    n_rows = s.shape[0]
    iota = lax.broadcasted_iota(jnp.int32, s.shape, 0)
    vals, idxs = [], []
    for _ in range(PEER_TOPK):
        m = jnp.max(s, axis=0, keepdims=True)
        am = jnp.min(jnp.where(s == m, iota, n_rows), axis=0, keepdims=True)
        sel = iota == am
        vals.append(m)
        idxs.append(am if cidx is None else jnp.sum(jnp.where(sel, cidx, 0), axis=0, keepdims=True))
        s = jnp.where(sel, -jnp.inf, s)
    return vals, idxs


def _peer_topk_body(q_ref, sk_ref, idx_ref, g_ref):
    def head(h, carry):
        tops = []
        for p in range(2):
            col = pl.multiple_of((h * 2 + p) * PEER_HALF, PEER_HALF)
            s = lax.dot_general(sk_ref[h * 2 + p], q_ref[:, pl.ds(col, PEER_HALF)], _NT,
                                preferred_element_type=F32)
            tops.append(_top_rows(s))
        (v1, i1), (v2, i2) = tops
        v2a = jnp.concatenate(v2, axis=0)
        i2a = jnp.concatenate(i2, axis=0)
        cand = jnp.concatenate([v1[i] + v2a for i in range(PEER_TOPK)], axis=0)
        cidx = jnp.concatenate([i1[i] * N_KEYS + i2a for i in range(PEER_TOPK)], axis=0)
        best, eidx = _top_rows(cand, cidx)
        b = jnp.concatenate(best, axis=0)
        e = jnp.exp(b - best[0])
        row = pl.multiple_of(h * PEER_TOPK, PEER_TOPK)
        idx_ref[pl.ds(row, PEER_TOPK), :] = jnp.concatenate(eidx, axis=0)
        g_ref[pl.ds(row, PEER_TOPK), :] = e / jnp.sum(e, axis=0, keepdims=True)
        return carry

    lax.fori_loop(0, PEER_HEADS, head, 0)


def peer_topk(q, subkeys, *, tb):
    t = q.shape[0]
    rows = PEER_HEADS * PEER_TOPK
    return pl.pallas_call(
        _peer_topk_body, grid=(t // tb,),
        in_specs=[pl.BlockSpec((tb, q.shape[1]), lambda i: (i, 0)),
                  pl.BlockSpec(subkeys.shape, lambda i: (0, 0, 0))],
        out_specs=[pl.BlockSpec((rows, tb), lambda i: (0, i)), pl.BlockSpec((rows, tb), lambda i: (0, i))],
        out_shape=[jax.ShapeDtypeStruct((rows, t), jnp.int32), jax.ShapeDtypeStruct((rows, t), F32)],
        compiler_params=_cparams(1), name="peer_topk")(q, subkeys)


PEER_ROWS = PEER_HEADS * PEER_TOPK
PEER_TB = 16


def _unpack_pairs(w):
    lo = lax.bitcast_convert_type(lax.shift_left(w, jnp.uint32(16)), F32)
    hi = lax.bitcast_convert_type(lax.bitwise_and(w, jnp.uint32(0xFFFF0000)), F32)
    return lo, hi


def _peer_gather_body(idx_ref, idx_next_ref, g_ref, hn_ref, uv_hbm, o_ref, buf_a, buf_b, sem_a, sem_b, *, n_steps):
    i = pl.program_id(0)
    half = hn_ref.shape[1] // 2
    n_chunk = half // 128
    blk = PEER_TB * PEER_ROWS

    def row_copy(e, buf, sem, t, j):
        return pltpu.make_async_copy(uv_hbm.at[e], buf.at[t, pl.ds(j, 1), :], sem)

    def wait_all(buf, sem):
        def wait_tok(t, c):
            for j in range(PEER_ROWS):
                row_copy(0, buf, sem, t, j).wait()
            return c
        lax.fori_loop(0, PEER_TB, wait_tok, 0)

    @pl.when(i == 0)
    def _():
        def tok(t, c):
            for j in range(PEER_ROWS):
                row_copy(idx_ref[0, t * PEER_ROWS + j], buf_a, sem_a, t, j).start()
            return c
        lax.fori_loop(0, PEER_TB, tok, 0)

    lane = lax.broadcasted_iota(jnp.int32, (PEER_ROWS, PEER_TB), 1)

    def process(k, buf, sem, buf_next, sem_next, ids_next_ref, ids_next_base):
        wait_all(buf, sem)
        r0 = k * PEER_TB
        act = jnp.zeros((PEER_ROWS, PEER_TB), F32)
        for t in range(PEER_TB):
            for j in range(PEER_ROWS):
                e = ids_next_ref[0, ids_next_base + t * PEER_ROWS + j]
                row_copy(e, buf_next, sem_next, t, j).start()
            acc = jnp.zeros((PEER_ROWS, 128), F32)
            for c in range(n_chunk):
                lo, hi = _unpack_pairs(buf[t, :, c * 128:(c + 1) * 128])
                acc = acc + lo * hn_ref[r0 + t:r0 + t + 1, c * 128:(c + 1) * 128]
                acc = acc + hi * hn_ref[r0 + t:r0 + t + 1, half + c * 128:half + (c + 1) * 128]
            act = jnp.where(lane == t, jnp.sum(acc, axis=-1, keepdims=True), act)
        gelu = 0.5 * act * (1.0 + lax.erf(act * np.float32(np.sqrt(0.5))))
        coef = g_ref[k] * gelu
        for t in range(PEER_TB):
            ct = coef[:, t:t + 1]
            for c in range(n_chunk):
                lo, hi = _unpack_pairs(buf[t, :, half + c * 128:half + (c + 1) * 128])
                o_ref[r0 + t:r0 + t + 1, c * 128:(c + 1) * 128] = jnp.sum(lo * ct, axis=0, keepdims=True)
                o_ref[r0 + t:r0 + t + 1, half + c * 128:half + (c + 1) * 128] = jnp.sum(hi * ct, axis=0, keepdims=True)

    process(0, buf_a, sem_a, buf_b, sem_b, idx_ref, blk)
    process(1, buf_b, sem_b, buf_a, sem_a, idx_next_ref, 0)

    @pl.when(i == n_steps - 1)
    def _():
        wait_all(buf_a, sem_a)


def peer_gather(idx, g, hn, uv):
    n_steps = idx.shape[0]
    t, d = hn.shape
    last = n_steps - 1
    ids = 2 * PEER_TB * PEER_ROWS
    return pl.pallas_call(
        functools.partial(_peer_gather_body, n_steps=n_steps), grid=(n_steps,),
        in_specs=[pl.BlockSpec((None, 1, ids), lambda i: (i, 0, 0), memory_space=pltpu.SMEM),
                  pl.BlockSpec((None, 1, ids), lambda i: (jnp.minimum(i + 1, last), 0, 0), memory_space=pltpu.SMEM),
                  pl.BlockSpec((2, PEER_ROWS, PEER_TB), lambda i: (i, 0, 0)),
                  pl.BlockSpec((2 * PEER_TB, d), lambda i: (i, 0)),
                  pl.BlockSpec(memory_space=pl.ANY)],
        out_specs=pl.BlockSpec((2 * PEER_TB, d), lambda i: (i, 0)),
        out_shape=jax.ShapeDtypeStruct((t, d), F32),
        scratch_shapes=[pltpu.VMEM((PEER_TB, PEER_ROWS, d), jnp.uint32), pltpu.VMEM((PEER_TB, PEER_ROWS, d), jnp.uint32),
                        pltpu.SemaphoreType.DMA, pltpu.SemaphoreType.DMA],
        compiler_params=_cparams(1), name="peer_gather")(idx, idx, g, hn, uv)


def _pack_pairs(a):
    b = lax.bitcast_convert_type(a.astype(BF16), jnp.uint16).astype(jnp.uint32)
    h = a.shape[1] // 2
    return b[:, :h] | (b[:, h:] << 16)


def _rope_tables(pos):
    half = QK_ROPE // 2
    freq = ROPE_THETA ** (-jnp.arange(half, dtype=F32) / half)
    ang = pos.astype(F32)[:, None] * freq[None, :]
    cos, sin = jnp.cos(ang), jnp.sin(ang)
    return jnp.concatenate([cos, cos], axis=-1), jnp.concatenate([sin, sin], axis=-1)


def _pad_lanes(t):
    return jnp.concatenate([t, jnp.zeros_like(t)], axis=-1)


def _rot_cols(w_rope, g_rope):
    half = QK_ROPE // 2
    return jnp.concatenate([-w_rope[..., half:] * g_rope[half:], w_rope[..., :half] * g_rope[:half]], axis=-1)


def _head_gain(g):
    return jnp.tile(jnp.concatenate([g, jnp.zeros((HEAD_PAD - QK_DIM,), F32)]), N_HEADS)


def _layer(x, p, hist, cache, wts, *, batch, seq, start_pos):
    t = x.shape[0]
    d_model = x.shape[1]
    q_lora = wts["q_norm_g"].shape[0]
    kv_lora = wts["kv_norm_g"].shape[0]
    pool_w = wts["pool_scale"].shape[0]
    bm = min(1024, t)
    ident = lambda acc: acc

    z = fused_mm(x, wts["w_in"], g1=wts["norm1_g"], epilogue=ident, out_dtype=F32, bm=bm, bn=1280, name="in_proj")
    off_u = q_lora + kv_lora
    off_ga = off_u + pool_w
    off_gb = off_ga + d_model
    off_pe = off_gb + d_model
    pe_raw = z[:, off_pe:off_pe + QK_ROPE]
    c_lat, ckv = latent_rows(z, wts["kv_norm_g"], c_blk=q_lora // kv_lora, pe_blk=off_pe // 128,
                             kv_lora=kv_lora, bm=min(512, t))

    pos = start_pos + jnp.arange(seq, dtype=jnp.int32)
    cos2, sin2 = _rope_tables(pos)
    n_rep = 1 if cache is None else batch
    ca, sa = jnp.tile(_pad_lanes(cos2), (n_rep, 1)), jnp.tile(_pad_lanes(sin2), (n_rep, 1))
    hb = min(512, t)
    q = head_mm(z, wts["w_q"], wts["q_gain"], ca, sa, scale=QK_DIM ** -0.5, bm=hb, bn=1024,
                g=wts["q_norm_g"], a_cols=(q_lora, 0), name="q_proj")

    if cache is None:
        k = head_mm(ckv, wts["w_k"], wts["k_gain"], ca, sa, scale=1.0, bm=hb, bn=1024, name="k_proj")
        v = fused_mm(ckv, wts["w_v"], a1_cols=(kv_lora, 0), epilogue=ident, out_dtype=BF16, bm=bm, bn=1024,
                     name="v_proj")
        o = flash_attention(q, k, v, batch=batch, seq=seq, tq=min(1024, seq), tk=min(512, seq))
    else:
        cache_c, cache_pe, page_table = cache
        past = page_table.shape[1] * cache_c.shape[1]
        qx = block_mm(q, wts["w_absorb"], out_dtype=BF16, bm=bm, name="q_absorb")
        xw = kv_lora + 2 * QK_ROPE
        qx = qx.reshape(batch, seq, N_HEADS, xw).transpose(0, 2, 1, 3).reshape(t * N_HEADS, xw)
        cos_k, sin_k = _rope_tables(jnp.arange(past, dtype=jnp.int32))
        pad = jnp.zeros((128 - seq, QK_ROPE), F32)
        cos_n, sin_n = jnp.concatenate([cos2, pad]), jnp.concatenate([sin2, pad])
        o_lat = sample_attention(qx, wts["w_ukT"], cache_c, cache_pe, page_table, cos_k, sin_k, cos_n, sin_n,
                                 c_lat, pe_raw, dec_seq=seq, pages_per_tile=min(8, page_table.shape[1]))
        o_lat = o_lat.reshape(batch, N_HEADS, seq, kv_lora).transpose(1, 0, 2, 3).reshape(N_HEADS, t, kv_lora)
        o = block_mm(o_lat, wts["w_uv_heads"], out_dtype=BF16, bm=bm, name="o_up")

    d = pool_diff(z, hist, u_blk=off_u // pool_w, width=pool_w, batch=batch, seq=seq,
                  tb=min(512, seq), start_pos=start_pos)
    y = block_mm(d, wts["w_pool"], out_dtype=BF16, bm=bm, col_scale=wts["pool_scale"], name="pool_map")

    def merge_epi(a, b, ga, gb):
        return jax.nn.sigmoid(ga) * a + jax.nn.sigmoid(gb) * b

    mm = fused_mm(o, wts["w_a"], a2=y, w2=wts["w_b"], epilogue=merge_epi, out_dtype=BF16, bm=bm, bn=1024,
                  extras=(("tile", z, off_ga // 1024), ("tile", z, off_gb // 1024)), name="merge")
    x1 = fused_mm(mm, wts["w_out"], epilogue=lambda acc, xr: xr + acc, out_dtype=F32, bm=bm, bn=1024,
                  extras=(("tile", x, 0),), name="out_proj")

    q_peer, hn = fused_mm(x1, wts["w_peer_q"], g1=wts["norm2_g"], epilogue=ident, out_dtype=BF16,
                          bm=min(512, t), bn=1024, side_dtype=F32, name="peer_q")
    idx_t, g_t = peer_topk(q_peer, wts["peer_subkeys"], tb=min(256, t))
    nb = t // PEER_TB
    idx = idx_t.T.reshape(nb // 2, 1, 2 * PEER_TB * PEER_ROWS)
    g = g_t.reshape(PEER_ROWS, nb, PEER_TB).transpose(1, 0, 2)
    x2 = x1 + peer_gather(idx, g, hn, wts["peer_uv"])

    def ple_epi(gate, emb, xr):
        return xr + emb * jax.nn.sigmoid(gate)

    x3 = fused_mm(x2, wts["w_ple_gate"], g1=wts["norm3_g"], a2=p, w2=wts["w_ple"], epilogue=ple_epi,
                  out_dtype=F32, bm=bm, bn=1024, extras=(("tile", x2, 0),), name="ple")
    u_pool = z[:, off_u:off_u + pool_w].reshape(batch, seq, pool_w)
    return x3, c_lat, pe_raw, u_pool


def _prep_weights(i, norm1_g, w_in, q_norm_g, w_q_up, kv_norm_g, w_kv_up, q_head_g, k_head_g, w_pool_map,
                  pool_scale, w_a_proj, w_b_proj, w_out, norm2_g, w_peer_q, peer_subkeys, peer_u, peer_v,
                  norm3_g, w_ple, w_ple_gate):
    d_model = w_in.shape[1]
    q_lora, kv_lora = q_norm_g.shape[1], kv_norm_g.shape[1]
    pool_w = pool_scale.shape[1]
    s0, s1, s2, s3 = q_lora, q_lora + kv_lora, q_lora + kv_lora + QK_ROPE, q_lora + kv_lora + QK_ROPE + pool_w
    wi = w_in[i]
    n_used = wi.shape[1]
    n_pad = -(-n_used // 1280) * 1280
    w_in_p = jnp.concatenate([wi[:, :s1], wi[:, s2:], wi[:, s1:s2], jnp.zeros((d_model, n_pad - n_used), F32)], axis=1)
    qg, kg = q_head_g[i], k_head_g[i]
    wq = w_q_up[i].reshape(q_lora, N_HEADS, QK_DIM)
    wq_r = wq[..., QK_NOPE:]
    w_q = jnp.concatenate([wq[..., :QK_NOPE], wq_r, _rot_cols(wq_r, qg[QK_NOPE:])], axis=-1).reshape(q_lora, -1)
    wkv = w_kv_up[i].reshape(kv_lora, N_HEADS, QK_NOPE + V_DIM)
    w_uk, w_uv = wkv[..., :QK_NOPE], wkv[..., QK_NOPE:]
    eye = jnp.eye(QK_ROPE, dtype=F32)
    pe_cols = jnp.concatenate([jnp.zeros((QK_ROPE, QK_NOPE), F32), eye, _rot_cols(eye, kg[QK_NOPE:])], axis=-1)
    top = jnp.concatenate([w_uk, jnp.zeros((kv_lora, N_HEADS, HEAD_PAD - QK_NOPE), F32)], axis=-1)
    mid = jnp.broadcast_to(pe_cols[:, None, :], (QK_ROPE, N_HEADS, HEAD_PAD))
    bot = jnp.zeros((128 - QK_ROPE, N_HEADS, HEAD_PAD), F32)
    w_k = jnp.concatenate([top, mid, bot], axis=0).reshape(kv_lora + 128, -1)
    g_kn, g_kr = kg[:QK_NOPE], kg[QK_NOPE:]
    half = QK_ROPE // 2
    absorb = jnp.transpose(w_uk, (1, 2, 0)) * g_kn[None, :, None]
    swap = jnp.concatenate([jnp.concatenate([jnp.zeros((half, half), F32), -jnp.eye(half, dtype=F32)], axis=1),
                            jnp.concatenate([jnp.eye(half, dtype=F32), jnp.zeros((half, half), F32)], axis=1)], axis=0)
    rope_map = jnp.concatenate([eye * g_kr[None, :], swap * g_kr[None, :]], axis=1)
    w_absorb = jnp.zeros((N_HEADS, HEAD_PAD, kv_lora + 2 * QK_ROPE), F32)
    w_absorb = w_absorb.at[:, :QK_NOPE, :kv_lora].set(absorb)
    w_absorb = w_absorb.at[:, QK_NOPE:QK_DIM, kv_lora:].set(jnp.broadcast_to(rope_map, (N_HEADS,) + rope_map.shape))
    return {
        "norm1_g": norm1_g[i], "w_in": w_in_p.astype(BF16), "q_norm_g": q_norm_g[i], "kv_norm_g": kv_norm_g[i],
        "w_q": w_q.astype(BF16), "q_gain": _head_gain(qg), "w_k": w_k.astype(BF16), "k_gain": _head_gain(kg),
        "w_v": w_uv.reshape(kv_lora, -1).astype(BF16),
        "w_ukT": jnp.transpose(w_uk, (1, 2, 0)).reshape(N_HEADS * QK_NOPE, kv_lora).astype(BF16),
        "w_uv_heads": jnp.transpose(w_uv, (1, 0, 2)).astype(BF16), "w_absorb": w_absorb.astype(BF16),
        "w_pool": w_pool_map[i].astype(BF16), "pool_scale": pool_scale[i],
        "w_a": w_a_proj[i].astype(BF16), "w_b": w_b_proj[i].astype(BF16), "w_out": w_out[i].astype(BF16),
        "norm2_g": norm2_g[i], "w_peer_q": w_peer_q[i].astype(BF16),
        "peer_subkeys": peer_subkeys[i].reshape(PEER_HEADS * 2, N_KEYS, PEER_HALF).astype(BF16),
        "peer_uv": jnp.concatenate([_pack_pairs(peer_u[i]), _pack_pairs(peer_v[i])], axis=1)[:, None, :],
        "norm3_g": norm3_g[i],
        "w_ple": w_ple[i].astype(BF16), "w_ple_gate": w_ple_gate[i].astype(BF16),
    }


def kernel(x_prompt, x_sample, p_prompt, p_sample, cache_kv_latent, cache_k_rope, state_pool, page_table, norm1_g, w_in, q_norm_g, w_q_up, kv_norm_g, w_kv_up, q_head_g, k_head_g, w_pool_map, pool_scale, w_a_proj, w_b_proj, w_out, norm2_g, w_peer_q, peer_subkeys, peer_u, peer_v, norm3_g, w_ple, w_ple_gate):
    b, s, d_model = x_prompt.shape
    db, ds, _ = x_sample.shape
    depth = w_in.shape[0]
    pool_w = pool_scale.shape[1]
    pool_pad = state_pool.shape[2]
    past_len = page_table.shape[1] * cache_kv_latent.shape[2]
    xp = x_prompt.reshape(b * s, d_model)
    xs = x_sample.reshape(db * ds, d_model)
    outs = [[] for _ in range(6)]
    for i in range(depth):
        wts = _prep_weights(i, norm1_g, w_in, q_norm_g, w_q_up, kv_norm_g, w_kv_up, q_head_g, k_head_g, w_pool_map,
                            pool_scale, w_a_proj, w_b_proj, w_out, norm2_g, w_peer_q, peer_subkeys, peer_u, peer_v,
                            norm3_g, w_ple, w_ple_gate)
        hist_p = jnp.zeros((b, POOL_HIST, pool_w), F32)
        xp, lat, pe, u = _layer(xp, p_prompt[i].reshape(b * s, -1), hist_p, None, wts, batch=b, seq=s, start_pos=0)
        outs[0].append(lat.reshape(b, s, -1))
        outs[1].append(pe.reshape(b, s, -1))
        outs[2].append(jnp.concatenate([hist_p[:, POOL_HIST - pool_pad:], u], axis=1)[:, -pool_pad:])
        hist_s = jnp.concatenate([jnp.zeros((db, POOL_HIST - pool_pad, pool_w), F32), state_pool[i]], axis=1)
        cache = (cache_kv_latent[i], cache_k_rope[i], page_table)
        xs, lat, pe, u = _layer(xs, p_sample[i].reshape(db * ds, -1), hist_s, cache, wts, batch=db, seq=ds,
                                start_pos=past_len)
        outs[3].append(lat.reshape(db, ds, -1))
        outs[4].append(pe.reshape(db, ds, -1))
        outs[5].append(jnp.concatenate([state_pool[i], u], axis=1)[:, -pool_pad:])
    return (xp.reshape(b, s, d_model), xs.reshape(db, ds, d_model)) + tuple(jnp.stack(o) for o in outs)
```
